```python
import math, functools
import jax, jax.numpy as jnp
from jax import lax
import numpy as np

D_MODEL = 1024
BATCH = 2
SEQ = 8192
DEPTH = 2
DEC_BATCH = 32
DEC_SEQ = 4
PAST_LEN = 8192
PAGE_SIZE = 128

W_CONV = D_MODEL // 2
W_ATT = D_MODEL
W_POOL = D_MODEL // 2
N_BRANCH = 3
N_HEADS = 8
HEAD_DIM = W_ATT // N_HEADS // 2
V_HEAD_DIM = 2 * HEAD_DIM
ROPE_THETA = 10000.0
Q_BLOCK = 128
CONV_WIDTH = 31
CONV_BUF = CONV_WIDTH - 1
POOL_WINDOWS = (2, 4, 8, 16)
N_POOL_GROUPS = 4
POOL_GROUP = W_POOL // N_POOL_GROUPS
POOL_MAX = 16
POOL_BUF = POOL_MAX - 1
EPS = 1e-6
_COL_SIZES = (W_CONV, W_CONV, W_CONV, W_ATT, W_ATT, W_ATT, W_ATT, W_POOL, W_POOL, D_MODEL, D_MODEL, D_MODEL)
D_IN = 3 * W_CONV + 4 * W_ATT + 2 * W_POOL + N_BRANCH * D_MODEL

kernel_name = 'hybrid_conv_diffattn_pool_decoder_step'


def rmsnorm(x, g):
    xf = x.astype(jnp.float32)
    y = xf * lax.rsqrt(jnp.mean(xf * xf, axis=-1, keepdims=True) + EPS)
    return (y * g.astype(jnp.float32)).astype(x.dtype)


def layernorm(x, g, b):
    xf = x.astype(jnp.float32)
    mu = jnp.mean(xf, axis=-1, keepdims=True)
    xc = xf - mu
    y = xc * lax.rsqrt(jnp.mean(xc * xc, axis=-1, keepdims=True) + EPS)
    return (y * g.astype(jnp.float32) + b.astype(jnp.float32)).astype(x.dtype)


def split_cols(z):
    idx = []
    acc = 0
    for sz in _COL_SIZES[:-1]:
        acc += sz
        idx.append(acc)
    return jnp.split(z, idx, axis=-1)


def rope(x, pos):
    half = HEAD_DIM // 2
    inv = ROPE_THETA ** (-jnp.arange(half, dtype=jnp.float32) / half)
    ang = pos.astype(jnp.float32)[:, None] * inv[None, :]
    cos = jnp.cos(ang)[:, None, :]
    sin = jnp.sin(ang)[:, None, :]
    xf = x.astype(jnp.float32)
    x1, x2 = xf[..., :half], xf[..., half:]
    out = jnp.concatenate([x1 * cos - x2 * sin, x2 * cos + x1 * sin], axis=-1)
    return out.astype(x.dtype)


def causal_dwconv(u_ext, w, b):
    y = lax.conv_general_dilated(u_ext, w[:, None, :].astype(u_ext.dtype), window_strides=(1,),
                                 padding='VALID', dimension_numbers=('NWC', 'WIO', 'NWC'),
                                 feature_group_count=u_ext.shape[-1])
    return y + b.astype(y.dtype)


def causal_multipool(u_ext, pos):
    L = u_ext.shape[1] - POOL_BUF
    uf = u_ext.astype(jnp.float32)
    csum = jnp.concatenate([jnp.zeros_like(uf[:, :1]), jnp.cumsum(uf, axis=1)], axis=1)
    outs = []
    for g, w in enumerate(POOL_WINDOWS):
        c0, c1 = g * POOL_GROUP, (g + 1) * POOL_GROUP
        end = csum[:, POOL_BUF + 1:POOL_BUF + 1 + L, c0:c1]
        start = csum[:, POOL_BUF + 1 - w:POOL_BUF + 1 - w + L, c0:c1]
        cnt = jnp.minimum(w, pos + 1).astype(jnp.float32)[None, :, None]
        outs.append((end - start) / cnt - uf[:, POOL_BUF:, c0:c1])
    return jnp.concatenate(outs, axis=-1).astype(u_ext.dtype)


def diff_core(q1, q2, k1, k2, v, mask, lam):
    scale = HEAD_DIM ** -0.5

    def probs(q, k):
        s = jnp.einsum('bqhd,bkhd->bhqk', q, k).astype(jnp.float32) * scale
        s = jnp.where(mask[None, None], s, -jnp.inf)
        return jax.nn.softmax(s, axis=-1)

    a = probs(q1, k1) - lam * probs(q2, k2)
    return jnp.einsum('bhqk,bkhd->bqhd', a.astype(v.dtype), v)


def diff_attn_prompt(q1, q2, k, v, lam):
    B, S = q1.shape[0], q1.shape[1]
    k1, k2 = k[..., :HEAD_DIM], k[..., HEAD_DIM:]
    kpos = jnp.arange(S)

    def block(i):
        s0 = i * Q_BLOCK
        qb1 = lax.dynamic_slice_in_dim(q1, s0, Q_BLOCK, axis=1)
        qb2 = lax.dynamic_slice_in_dim(q2, s0, Q_BLOCK, axis=1)
        qpos = s0 + jnp.arange(Q_BLOCK)
        mask = kpos[None, :] <= qpos[:, None]
        return diff_core(qb1, qb2, k1, k2, v, mask, lam)

    out = lax.map(block, jnp.arange(S // Q_BLOCK))
    return jnp.transpose(out, (1, 0, 2, 3, 4)).reshape(B, S, N_HEADS, V_HEAD_DIM)


def diff_attn_sample(q1, q2, k, v, lam, k_past, v_past):
    P, T = k_past.shape[1], q1.shape[1]
    k_all = jnp.concatenate([k_past.astype(k.dtype), k], axis=1)
    v_all = jnp.concatenate([v_past.astype(v.dtype), v], axis=1)
    mask = jnp.arange(P + T)[None, :] <= (P + jnp.arange(T))[:, None]
    return diff_core(q1, q2, k_all[..., :HEAD_DIM], k_all[..., HEAD_DIM:], v_all, mask, lam)


def mixer_layer(x, pos, conv_prev, pool_prev, attend, lam_init, p):
    B, L, _ = x.shape
    h = rmsnorm(x, p['norm_g'])
    z = h @ p['w_in']
    (c_val, c_glu, c_gate, q, k, v, a_gate, pl_in, pl_gate, g_conv, g_att, g_pool) = split_cols(z)
    u = c_val * jax.nn.sigmoid(c_glu)
    u_ext = jnp.concatenate([conv_prev.astype(u.dtype), u], axis=1)
    c = causal_dwconv(u_ext, p['conv_w'], p['conv_b'])
    c = jax.nn.silu(layernorm(c, p['conv_ln_g'], p['conv_ln_b']))
    c = (c @ p['conv_pw']) * jax.nn.silu(c_gate)
    br_conv = c @ p['w_conv_up']
    q = rope(q.reshape(B, L, 2 * N_HEADS, HEAD_DIM), pos).reshape(B, L, N_HEADS, 2, HEAD_DIM)
    k = rope(k.reshape(B, L, 2 * N_HEADS, HEAD_DIM), pos).reshape(B, L, N_HEADS, 2 * HEAD_DIM)
    v = v.reshape(B, L, N_HEADS, V_HEAD_DIM)
    f32 = jnp.float32
    lam = (jnp.exp(jnp.sum(p['lq1'].astype(f32) * p['lk1'].astype(f32)))
           - jnp.exp(jnp.sum(p['lq2'].astype(f32) * p['lk2'].astype(f32))) + lam_init)
    o = attend(q[..., 0, :], q[..., 1, :], k, v, lam)
    o = rmsnorm(o, p['subln_g']) * (1.0 - lam_init)
    o = o.reshape(B, L, W_ATT) * jax.nn.silu(a_gate)
    br_att = o @ p['w_att_up']
    pl_ext = jnp.concatenate([pool_prev.astype(pl_in.dtype), pl_in], axis=1)
    m = causal_multipool(pl_ext, pos).reshape(B, L, N_POOL_GROUPS, POOL_GROUP)
    m = jnp.einsum('blgc,gcd->blgd', m, p['pool_w']).reshape(B, L, W_POOL) + p['pool_b']
    m = m * p['pool_scale'] * jax.nn.silu(pl_gate)
    br_pool = m @ p['w_pool_up']
    merged = (jax.nn.sigmoid(g_conv) * br_conv + jax.nn.sigmoid(g_att) * br_att
              + jax.nn.sigmoid(g_pool) * br_pool)
    y = x + merged @ p['w_out']
    return y, k, v, u_ext[:, -CONV_BUF:], pl_ext[:, -POOL_BUF:]


def setup_inputs(seed: int = 0) -> dict:
    key = jax.random.key(seed)
    ks = jax.random.split(key, 32)
    f32 = jnp.float32
    n_pages = PAST_LEN // PAGE_SIZE
    n_used = DEC_BATCH * n_pages
    n_pool = n_used + n_used // 4

    def nrm(k, shape, scale):
        return jax.random.normal(k, shape, f32) * scale

    page_table = jax.random.permutation(ks[0], n_pool)[:n_used].reshape(DEC_BATCH, n_pages).astype(jnp.int32)
    return {
        'x_prompt': nrm(ks[1], (BATCH, SEQ, D_MODEL), 1.0),
        'x_sample': nrm(ks[2], (DEC_BATCH, DEC_SEQ, D_MODEL), 1.0),
        'cache_k': nrm(ks[3], (DEPTH, n_pool, PAGE_SIZE, N_HEADS, 2 * HEAD_DIM), 1.0),
        'cache_v': nrm(ks[4], (DEPTH, n_pool, PAGE_SIZE, N_HEADS, V_HEAD_DIM), 1.0),
        'state_conv': nrm(ks[5], (DEPTH, DEC_BATCH, CONV_BUF, W_CONV), 0.5),
        'state_pool': nrm(ks[6], (DEPTH, DEC_BATCH, POOL_BUF, W_POOL), 1.0),
        'page_table': page_table,
        'norm_g': 1.0 + nrm(ks[7], (DEPTH, D_MODEL), 0.02),
        'w_in': nrm(ks[8], (DEPTH, D_MODEL, D_IN), D_MODEL ** -0.5),
        'conv_w': nrm(ks[9], (DEPTH, CONV_WIDTH, W_CONV), CONV_WIDTH ** -0.5),
        'conv_b': nrm(ks[10], (DEPTH, W_CONV), 0.02),
        'conv_ln_g': 1.0 + nrm(ks[11], (DEPTH, W_CONV), 0.02),
        'conv_ln_b': nrm(ks[12], (DEPTH, W_CONV), 0.02),
        'conv_pw': nrm(ks[13], (DEPTH, W_CONV, W_CONV), W_CONV ** -0.5),
        'lambda_q1': nrm(ks[14], (DEPTH, HEAD_DIM), 0.1),
        'lambda_k1': nrm(ks[15], (DEPTH, HEAD_DIM), 0.1),
        'lambda_q2': nrm(ks[16], (DEPTH, HEAD_DIM), 0.1),
        'lambda_k2': nrm(ks[17], (DEPTH, HEAD_DIM), 0.1),
        'subln_g': 1.0 + nrm(ks[18], (DEPTH, V_HEAD_DIM), 0.02),
        'pool_w': nrm(ks[19], (DEPTH, N_POOL_GROUPS, POOL_GROUP, POOL_GROUP), POOL_GROUP ** -0.5),
        'pool_b': nrm(ks[20], (DEPTH, W_POOL), 0.02),
        'pool_scale': 1.0 + nrm(ks[21], (DEPTH, W_POOL), 0.05),
        'w_conv_up': nrm(ks[22], (DEPTH, W_CONV, D_MODEL), W_CONV ** -0.5),
        'w_att_up': nrm(ks[23], (DEPTH, W_ATT, D_MODEL), W_ATT ** -0.5),
        'w_pool_up': nrm(ks[24], (DEPTH, W_POOL, D_MODEL), W_POOL ** -0.5),
        'w_out': nrm(ks[25], (DEPTH, D_MODEL, D_MODEL), D_MODEL ** -0.5),
        'final_g': 1.0 + nrm(ks[26], (D_MODEL,), 0.02),
    }


def reference(x_prompt, x_sample, cache_k, cache_v, state_conv, state_pool, page_table,
              norm_g, w_in, conv_w, conv_b, conv_ln_g, conv_ln_b, conv_pw,
              lambda_q1, lambda_k1, lambda_q2, lambda_k2, subln_g,
              pool_w, pool_b, pool_scale, w_conv_up, w_att_up, w_pool_up, w_out, final_g):
    bp, s, _ = x_prompt.shape
    bd, t, _ = x_sample.shape
    past = page_table.shape[1] * cache_k.shape[2]
    pos_p = jnp.arange(s, dtype=jnp.int32)
    pos_s = past + jnp.arange(t, dtype=jnp.int32)
    xp, xs = x_prompt, x_sample
    kp, vp, cp, pp = [], [], [], []
    ksm, vsm, csm, psm = [], [], [], []
    for l in range(DEPTH):
        p = dict(norm_g=norm_g[l], w_in=w_in[l], conv_w=conv_w[l], conv_b=conv_b[l],
                 conv_ln_g=conv_ln_g[l], conv_ln_b=conv_ln_b[l], conv_pw=conv_pw[l],
                 lq1=lambda_q1[l], lk1=lambda_k1[l], lq2=lambda_q2[l], lk2=lambda_k2[l],
                 subln_g=subln_g[l], pool_w=pool_w[l], pool_b=pool_b[l], pool_scale=pool_scale[l],
                 w_conv_up=w_conv_up[l], w_att_up=w_att_up[l], w_pool_up=w_pool_up[l], w_out=w_out[l])
        lam_init = 0.8 - 0.6 * math.exp(-0.3 * l)
        conv0 = jnp.zeros((bp, CONV_BUF, W_CONV), xp.dtype)
        pool0 = jnp.zeros((bp, POOL_BUF, W_POOL), xp.dtype)
        xp, k_n, v_n, c_n, pl_n = mixer_layer(xp, pos_p, conv0, pool0, diff_attn_prompt, lam_init, p)
        kp.append(k_n)
        vp.append(v_n)
        cp.append(c_n)
        pp.append(pl_n)
        k_past = cache_k[l, page_table].reshape(bd, past, N_HEADS, 2 * HEAD_DIM)
        v_past = cache_v[l, page_table].reshape(bd, past, N_HEADS, V_HEAD_DIM)
        attend_s = functools.partial(diff_attn_sample, k_past=k_past, v_past=v_past)
        xs, k_n, v_n, c_n, pl_n = mixer_layer(xs, pos_s, state_conv[l], state_pool[l], attend_s, lam_init, p)
        ksm.append(k_n)
        vsm.append(v_n)
        csm.append(c_n)
        psm.append(pl_n)
    y_prompt = rmsnorm(xp, final_g)
    y_sample = rmsnorm(xs, final_g)
    return (y_prompt, y_sample, jnp.stack(kp), jnp.stack(vp), jnp.stack(cp), jnp.stack(pp),
            jnp.stack(ksm), jnp.stack(vsm), jnp.stack(csm), jnp.stack(psm))
```

```python
import functools
import math

import jax
import jax.numpy as jnp
from jax import lax
from jax.experimental import pallas as pl
from jax.experimental.pallas import tpu as pltpu

F32 = jnp.float32
BF16 = jnp.bfloat16

D_MODEL = 1024
W_CONV = 512
W_ATT = 1024
W_POOL = 512
N_HEADS = 8
HEAD_DIM = 64
V_HEAD_DIM = 128
ROPE_THETA = 10000.0
CONV_WIDTH = 31
CONV_BUF = CONV_WIDTH - 1
POOL_WINDOWS = (2, 4, 8, 16)
POOL_GROUP = 128
POOL_MAX = 16
POOL_BUF = POOL_MAX - 1
EPS = 1e-6
LANES = 128
CONV_HALO = 32
POOL_HALO = 16
VMEM_LIMIT = 56 * 1024 * 1024

_C_VAL, _C_GLU, _C_GATE = 0, 512, 1024
_Q, _K, _V, _A_GATE = 1536, 2560, 3584, 4608
_PL_IN, _PL_GATE = 5632, 6144
_G_CONV, _G_ATT, _G_POOL = 6656, 7680, 8704


def _dot(a, b):
    return jnp.dot(a, b, preferred_element_type=F32)


def _dot_nt(a, b):
    return lax.dot_general(a, b, (((1,), (1,)), ((), ())), preferred_element_type=F32)


def _sigmoid(x):
    return 1.0 / (1.0 + jnp.exp(-x))


def _silu(x):
    return x * _sigmoid(x)


def _rms_rows(x, g):
    return x * lax.rsqrt(jnp.mean(x * x, axis=-1, keepdims=True) + EPS) * g


def _const_spec(shape):
    nd = len(shape)
    return pl.BlockSpec(shape, lambda *_: (0,) * nd, pipeline_mode=pl.Buffered(1))


def _params(*sem):
    return pltpu.CompilerParams(dimension_semantics=sem, vmem_limit_bytes=VMEM_LIMIT)


def _qkv_kernel(x_ref, g_ref, w_ref, cos_ref, sin_ref,
                q_ref, k_ref, kb_ref, v_ref, vb_ref, ag_ref, sg_ref):
    h = _rms_rows(x_ref[...], g_ref[...]).astype(BF16)
    cos = cos_ref[...]
    sin = sin_ref[...]
    first_half = (lax.broadcasted_iota(jnp.int32, cos.shape, 1) % HEAD_DIM) < (HEAD_DIM // 2)

    def rope(z):
        partner = jnp.where(first_half, pltpu.roll(z, LANES - HEAD_DIM // 2, 1),
                            pltpu.roll(z, HEAD_DIM // 2, 1))
        return z * cos + partner * sin

    scale = HEAD_DIM ** -0.5
    zq = _dot(h, w_ref[:, 0:W_ATT])
    for c in range(W_ATT // LANES):
        sl = slice(c * LANES, (c + 1) * LANES)
        q_ref[:, sl] = (rope(zq[:, sl]) * scale).astype(BF16)
    zk = _dot(h, w_ref[:, W_ATT:2 * W_ATT])
    for c in range(W_ATT // LANES):
        sl = slice(c * LANES, (c + 1) * LANES)
        kr = rope(zk[:, sl])
        k_ref[:, sl] = kr
        kb_ref[:, sl] = kr.astype(BF16)
    zv = _dot(h, w_ref[:, 2 * W_ATT:3 * W_ATT])
    v_ref[...] = zv
    vb_ref[...] = zv.astype(BF16)
    ag_ref[...] = _silu(_dot(h, w_ref[:, 3 * W_ATT:4 * W_ATT])).astype(BF16)
    sg_ref[...] = _sigmoid(_dot(h, w_ref[:, 4 * W_ATT:5 * W_ATT])).astype(BF16)


def _qkv_proj(x, norm_g, w_qkv, cos_tab, sin_tab, tm):
    nb, n, _ = x.shape
    row = lambda b, i: (b, i, 0)
    tab = lambda b, i: (i, 0)
    blk = pl.BlockSpec((None, tm, D_MODEL), row)
    shp = lambda dt: jax.ShapeDtypeStruct((nb, n, W_ATT), dt)
    return pl.pallas_call(
        _qkv_kernel,
        grid=(nb, n // tm),
        in_specs=[blk, _const_spec((1, D_MODEL)), _const_spec(w_qkv.shape),
                  pl.BlockSpec((tm, LANES), tab), pl.BlockSpec((tm, LANES), tab)],
        out_specs=[blk] * 7,
        out_shape=[shp(BF16), shp(F32), shp(BF16), shp(F32), shp(BF16), shp(BF16), shp(BF16)],
        compiler_params=_params("parallel", "parallel"),
        name="qkv_proj",
    )(x, norm_g, w_qkv, cos_tab, sin_tab)


def _conv_pool_kernel(x_ref, g_ref, w_ref, cinit_ref, pinit_ref,
                      cw_ref, cb_ref, lng_ref, lnb_ref, pw_ref, cup_ref,
                      poolw_ref, poolb_ref, pscale_ref, pup_ref,
                      mcp_ref, cstate_ref, pstate_ref, uext, pext, *, tm, nt, pos_base):
    t = pl.program_id(1)

    @pl.when(t == 0)
    def _():
        uext[0:CONV_HALO - CONV_BUF, :] = jnp.zeros((CONV_HALO - CONV_BUF, W_CONV), F32)
        uext[CONV_HALO - CONV_BUF:CONV_HALO, :] = cinit_ref[...]
        pext[0:POOL_HALO - POOL_BUF, :] = jnp.zeros((POOL_HALO - POOL_BUF, W_POOL), F32)
        pext[POOL_HALO - POOL_BUF:POOL_HALO, :] = pinit_ref[...]

    h = _rms_rows(x_ref[...], g_ref[...]).astype(BF16)

    def proj(off, width):
        return _dot(h, w_ref[:, off:off + width])

    u = proj(0, W_CONV) * _sigmoid(proj(W_CONV, W_CONV))
    uext[CONV_HALO:CONV_HALO + tm, :] = u
    first = CONV_HALO - CONV_BUF
    acc = jnp.zeros((tm, W_CONV), F32) + cb_ref[...]
    for j in range(CONV_WIDTH):
        acc = acc + uext[first + j:first + j + tm, :] * cw_ref[j:j + 1, :]
    mu = jnp.mean(acc, axis=-1, keepdims=True)
    xc = acc - mu
    c = xc * lax.rsqrt(jnp.mean(xc * xc, axis=-1, keepdims=True) + EPS) * lng_ref[...] + lnb_ref[...]
    c = _silu(c).astype(BF16)
    c = _dot(c, pw_ref[...]) * _silu(proj(2 * W_CONV, W_CONV))
    br_conv = _dot(c.astype(BF16), cup_ref[...])

    off_pool = 3 * W_CONV
    pl_in = proj(off_pool, W_POOL)
    pext[POOL_HALO:POOL_HALO + tm, :] = pl_in
    pos = pos_base + t * tm + lax.broadcasted_iota(jnp.int32, (tm, POOL_GROUP), 0)
    pooled = []
    for g, w in enumerate(POOL_WINDOWS):
        cols = slice(g * POOL_GROUP, (g + 1) * POOL_GROUP)
        tok = pext[POOL_HALO:POOL_HALO + tm, cols]
        win = tok
        for i in range(1, w):
            win = win + pext[POOL_HALO - i:POOL_HALO - i + tm, cols]
        cnt = jnp.minimum(w, pos + 1).astype(F32)
        m = (win / cnt - tok).astype(BF16)
        pooled.append(_dot(m, poolw_ref[g]))
    m = jnp.concatenate(pooled, axis=-1) + poolb_ref[...]
    m = m * pscale_ref[...] * _silu(proj(off_pool + W_POOL, W_POOL))
    br_pool = _dot(m.astype(BF16), pup_ref[...])

    off_gate = off_pool + 2 * W_POOL
    mcp_ref[...] = (_sigmoid(proj(off_gate, D_MODEL)) * br_conv
                    + _sigmoid(proj(off_gate + D_MODEL, D_MODEL)) * br_pool)

    @pl.when(t == nt - 1)
    def _():
        cstate_ref[...] = uext[tm + CONV_HALO - CONV_BUF:tm + CONV_HALO, :]
        pstate_ref[...] = pext[tm + POOL_HALO - POOL_BUF:tm + POOL_HALO, :]

    if nt > 1:
        uext[0:CONV_HALO, :] = uext[tm:tm + CONV_HALO, :]
        pext[0:POOL_HALO, :] = pext[tm:tm + POOL_HALO, :]


def _conv_pool(x, norm_g, w_cp, conv_init, pool_init, p, tm, pos_base):
    nb, n, _ = x.shape
    nt = n // tm
    row = lambda b, i: (b, i, 0)
    per_b = lambda b, i: (b, 0, 0)
    consts = [p['conv_w'], p['conv_b'], p['conv_ln_g'], p['conv_ln_b'], p['conv_pw'], p['w_conv_up'],
              p['pool_w'], p['pool_b'], p['pool_scale'], p['w_pool_up']]
    return pl.pallas_call(
        functools.partial(_conv_pool_kernel, tm=tm, nt=nt, pos_base=pos_base),
        grid=(nb, nt),
        in_specs=[pl.BlockSpec((None, tm, D_MODEL), row), _const_spec((1, D_MODEL)), _const_spec(w_cp.shape),
                  pl.BlockSpec((None, CONV_BUF, W_CONV), per_b), pl.BlockSpec((None, POOL_BUF, W_POOL), per_b)]
                 + [_const_spec(c.shape) for c in consts],
        out_specs=[pl.BlockSpec((None, tm, D_MODEL), row),
                   pl.BlockSpec((None, CONV_BUF, W_CONV), per_b), pl.BlockSpec((None, POOL_BUF, W_POOL), per_b)],
        out_shape=[jax.ShapeDtypeStruct((nb, n, D_MODEL), F32),
                   jax.ShapeDtypeStruct((nb, CONV_BUF, W_CONV), F32),
                   jax.ShapeDtypeStruct((nb, POOL_BUF, W_POOL), F32)],
        scratch_shapes=[pltpu.VMEM((CONV_HALO + tm, W_CONV), F32), pltpu.VMEM((POOL_HALO + tm, W_POOL), F32)],
        compiler_params=_params("parallel", "arbitrary"),
        name="conv_pool",
    )(x, norm_g, w_cp, conv_init, pool_init, *consts)


def _lambda(lam_ref, lam_init):
    lp = lam_ref[...]
    return (jnp.exp(jnp.sum(lp[0:1] * lp[1:2], axis=-1, keepdims=True))
            - jnp.exp(jnp.sum(lp[2:3] * lp[3:4], axis=-1, keepdims=True)) + lam_init)


def _split_branches(q):
    lane = lax.broadcasted_iota(jnp.int32, q.shape, 1)
    zero = jnp.zeros_like(q)
    return jnp.concatenate([jnp.where(lane < HEAD_DIM, q, zero), jnp.where(lane >= HEAD_DIM, q, zero)], axis=0)


def _online_softmax_step(s, v, m_scr, l_scr, acc_scr):
    m_prev = m_scr[...]
    m_new = jnp.maximum(m_prev, jnp.max(s, axis=-1, keepdims=True))
    alpha = jnp.exp(m_prev - m_new)
    p = jnp.exp(s - m_new)
    l_scr[...] = alpha * l_scr[...] + jnp.sum(p, axis=-1, keepdims=True)
    acc_scr[...] = alpha * acc_scr[...] + _dot(p.astype(BF16), v)
    m_scr[...] = m_new


def _prompt_attn_kernel(lam_ref, q_ref, k_ref, v_ref, o_ref, m_scr, l_scr, acc_scr, *, tq, tk, lam_init):
    qi = pl.program_id(2)
    q2 = _split_branches(q_ref[...])
    m_scr[...] = jnp.full(m_scr.shape, -jnp.inf, F32)
    l_scr[...] = jnp.zeros(l_scr.shape, F32)
    acc_scr[...] = jnp.zeros(acc_scr.shape, F32)

    def kv_block(j):
        start = pl.multiple_of(j * tk, tk)
        return k_ref[pl.ds(start, tk), :], v_ref[pl.ds(start, tk), :]

    def full_block(j, carry):
        k, v = kv_block(j)
        _online_softmax_step(_dot_nt(q2, k), v, m_scr, l_scr, acc_scr)
        return carry

    q_start = qi * tq
    diag = q_start // tk
    lax.fori_loop(0, diag, full_block, 0)

    k, v = kv_block(diag)
    s = _dot_nt(q2, k)
    qpos = q_start + lax.broadcasted_iota(jnp.int32, (2 * tq, tk), 0) % tq
    kpos = diag * tk + lax.broadcasted_iota(jnp.int32, (2 * tq, tk), 1)
    _online_softmax_step(jnp.where(kpos <= qpos, s, -jnp.inf), v, m_scr, l_scr, acc_scr)

    o = acc_scr[...] / l_scr[...]
    o_ref[...] = o[:tq] - _lambda(lam_ref, lam_init) * o[tq:]


def _prompt_attention(lam_p, q, k, v, lam_init, tq, tk):
    nb, s, _ = q.shape
    return pl.pallas_call(
        functools.partial(_prompt_attn_kernel, tq=tq, tk=tk, lam_init=lam_init),
        grid=(nb, N_HEADS, s // tq),
        in_specs=[_const_spec(lam_p.shape),
                  pl.BlockSpec((None, tq, V_HEAD_DIM), lambda b, h, i: (b, i, h)),
                  pl.BlockSpec((None, s, V_HEAD_DIM), lambda b, h, i: (b, 0, h)),
                  pl.BlockSpec((None, s, V_HEAD_DIM), lambda b, h, i: (b, 0, h))],
        out_specs=pl.BlockSpec((None, tq, V_HEAD_DIM), lambda b, h, i: (b, i, h)),
        out_shape=jax.ShapeDtypeStruct((nb, s, W_ATT), F32),
        scratch_shapes=[pltpu.VMEM((2 * tq, 1), F32), pltpu.VMEM((2 * tq, 1), F32),
                        pltpu.VMEM((2 * tq, V_HEAD_DIM), F32)],
        compiler_params=_params("parallel", "parallel", "arbitrary"),
        name="prompt_attention",
    )(lam_p, q, k, v)


def _sample_attn_kernel(pt_ref, lam_ref, q_ref, kn_ref, vn_ref, kc_ref, vc_ref, o_ref,
                        m_scr, l_scr, acc_scr, *, n_pages, page, t_new, lam_init):
    j = pl.program_id(1)
    rows = 2 * t_new * N_HEADS
    q2 = _split_branches(q_ref[...])

    @pl.when(j == 0)
    def _():
        m_scr[...] = jnp.full(m_scr.shape, -jnp.inf, F32)
        l_scr[...] = jnp.zeros(l_scr.shape, F32)
        acc_scr[...] = jnp.zeros(acc_scr.shape, F32)

    n_kv = page * N_HEADS
    k = kc_ref[...].reshape(n_kv, V_HEAD_DIM).astype(BF16)
    v = vc_ref[...].reshape(n_kv, V_HEAD_DIM).astype(BF16)
    row_head = lax.broadcasted_iota(jnp.int32, (rows, n_kv), 0) % N_HEADS
    col_head = lax.broadcasted_iota(jnp.int32, (rows, n_kv), 1) % N_HEADS
    s = jnp.where(row_head == col_head, _dot_nt(q2, k), -jnp.inf)
    _online_softmax_step(s, v, m_scr, l_scr, acc_scr)

    @pl.when(j == n_pages - 1)
    def _():
        n_new = t_new * N_HEADS
        kn = kn_ref[...].astype(BF16)
        vn = vn_ref[...].astype(BF16)
        r = lax.broadcasted_iota(jnp.int32, (rows, n_new), 0)
        c = lax.broadcasted_iota(jnp.int32, (rows, n_new), 1)
        ok = (r % N_HEADS == c % N_HEADS) & (c // N_HEADS <= (r // N_HEADS) % t_new)
        _online_softmax_step(jnp.where(ok, _dot_nt(q2, kn), -jnp.inf), vn, m_scr, l_scr, acc_scr)
        o = acc_scr[...] / l_scr[...]
        o_ref[...] = o[:n_new] - _lambda(lam_ref, lam_init) * o[n_new:]


def _sample_attention(page_table, lam_p, q, k_new, v_new, cache_k, cache_v, layer, lam_init):
    nb, n_pages = page_table.shape
    page = cache_k.shape[2]
    n_new = q.shape[1]
    t_new = n_new // N_HEADS
    per_b = lambda b, j, pt: (b, 0, 0)
    paged = lambda b, j, pt: (layer, pt[b, j], 0, 0, 0)
    new_spec = pl.BlockSpec((None, n_new, V_HEAD_DIM), per_b)
    page_spec = pl.BlockSpec((None, None, page, N_HEADS, V_HEAD_DIM), paged)
    return pl.pallas_call(
        functools.partial(_sample_attn_kernel, n_pages=n_pages, page=page, t_new=t_new, lam_init=lam_init),
        grid_spec=pltpu.PrefetchScalarGridSpec(
            num_scalar_prefetch=1,
            grid=(nb, n_pages),
            in_specs=[pl.BlockSpec(lam_p.shape, lambda b, j, pt: (0, 0)),
                      new_spec, new_spec, new_spec, page_spec, page_spec],
            out_specs=new_spec,
            scratch_shapes=[pltpu.VMEM((2 * n_new, 1), F32), pltpu.VMEM((2 * n_new, 1), F32),
                            pltpu.VMEM((2 * n_new, V_HEAD_DIM), F32)]),
        out_shape=jax.ShapeDtypeStruct((nb, n_new, V_HEAD_DIM), F32),
        compiler_params=_params("parallel", "arbitrary"),
        name="sample_attention",
    )(page_table, lam_p, q, k_new, v_new, cache_k, cache_v)


def _out_kernel(o_ref, ag_ref, sg_ref, mcp_ref, x_ref, subg_ref, watt_ref, wout_ref, fg_ref, y_ref,
                *, lam_init, final):
    heads = []
    for hd in range(N_HEADS):
        oh = o_ref[:, hd * V_HEAD_DIM:(hd + 1) * V_HEAD_DIM]
        heads.append(_rms_rows(oh, subg_ref[...]) * (1.0 - lam_init))
    o = jnp.concatenate(heads, axis=-1) * ag_ref[...].astype(F32)
    br_att = _dot(o.astype(BF16), watt_ref[...])
    merged = sg_ref[...].astype(F32) * br_att + mcp_ref[...]
    y = x_ref[...] + _dot(merged.astype(BF16), wout_ref[...])
    y_ref[...] = _rms_rows(y, fg_ref[...]) if final else y


def _out_proj(o, ag, sg, mcp, x, subln_g, w_att_up, w_out, final_g, lam_init, final, tm):
    n = x.shape[0]
    row = lambda i: (i, 0)
    blk = pl.BlockSpec((tm, D_MODEL), row)
    return pl.pallas_call(
        functools.partial(_out_kernel, lam_init=lam_init, final=final),
        grid=(n // tm,),
        in_specs=[blk] * 5 + [_const_spec(subln_g.shape), _const_spec(w_att_up.shape),
                              _const_spec(w_out.shape), _const_spec(final_g.shape)],
        out_specs=blk,
        out_shape=jax.ShapeDtypeStruct((n, D_MODEL), F32),
        compiler_params=_params("parallel"),
        name="out_proj",
    )(o, ag, sg, mcp, x, subln_g, w_att_up, w_out, final_g)


def _rope_tables(pos):
    half = HEAD_DIM // 2
    inv = ROPE_THETA ** (-jnp.arange(half, dtype=F32) / half)
    ang = pos.astype(F32)[:, None] * inv[None, :]
    cos, sin = jnp.cos(ang), jnp.sin(ang)
    reps = LANES // HEAD_DIM
    return (jnp.tile(jnp.concatenate([cos, cos], axis=-1), (1, reps)),
            jnp.tile(jnp.concatenate([-sin, sin], axis=-1), (1, reps)))


def kernel(x_prompt, x_sample, cache_k, cache_v, state_conv, state_pool, page_table, norm_g, w_in, conv_w, conv_b,
           conv_ln_g, conv_ln_b, conv_pw, lambda_q1, lambda_k1, lambda_q2, lambda_k2, subln_g, pool_w, pool_b,
           pool_scale, w_conv_up, w_att_up, w_pool_up, w_out, final_g):
    bp, s, _ = x_prompt.shape
    bd, t, _ = x_sample.shape
    depth = w_in.shape[0]
    past = page_table.shape[1] * cache_k.shape[2]
    cos_p, sin_p = _rope_tables(jnp.arange(s, dtype=jnp.int32))
    cos_s, sin_s = _rope_tables(jnp.tile(past + jnp.arange(t, dtype=jnp.int32), bd))
    conv0 = jnp.zeros((bp, CONV_BUF, W_CONV), F32)
    pool0 = jnp.zeros((bp, POOL_BUF, W_POOL), F32)
    fg = final_g.reshape(1, D_MODEL)

    xp, xs = x_prompt, x_sample
    outs = [[] for _ in range(8)]
    for l in range(depth):
        lam_init = 0.8 - 0.6 * math.exp(-0.3 * l)
        final = l == depth - 1
        wl = w_in[l]
        w_qkv = jnp.concatenate([wl[:, _Q:_PL_IN], wl[:, _G_ATT:_G_POOL]], axis=1).astype(BF16)
        w_cp = jnp.concatenate([wl[:, _C_VAL:_Q], wl[:, _PL_IN:_G_ATT], wl[:, _G_POOL:]], axis=1).astype(BF16)
        ng = norm_g[l].reshape(1, D_MODEL)
        lam_p = jnp.stack([lambda_q1[l], lambda_k1[l], lambda_q2[l], lambda_k2[l]])
        p = dict(conv_w=conv_w[l], conv_b=conv_b[l].reshape(1, -1), conv_ln_g=conv_ln_g[l].reshape(1, -1),
                 conv_ln_b=conv_ln_b[l].reshape(1, -1), conv_pw=conv_pw[l].astype(BF16),
                 w_conv_up=w_conv_up[l].astype(BF16), pool_w=pool_w[l].astype(BF16),
                 pool_b=pool_b[l].reshape(1, -1), pool_scale=pool_scale[l].reshape(1, -1),
                 w_pool_up=w_pool_up[l].astype(BF16))
        sub_g = subln_g[l].reshape(1, V_HEAD_DIM)
        watt = w_att_up[l].astype(BF16)
        wo = w_out[l].astype(BF16)

        q, k, kb, v, vb, ag, sg = _qkv_proj(xp, ng, w_qkv, cos_p, sin_p, tm=512)
        mcp, c_n, pl_n = _conv_pool(xp, ng, w_cp, conv0, pool0, p, tm=512, pos_base=0)
        o = _prompt_attention(lam_p, q, kb, vb, lam_init, tq=256, tk=512)
        flat = lambda a: a.reshape(bp * s, D_MODEL)
        xp = _out_proj(flat(o), flat(ag), flat(sg), flat(mcp), flat(xp), sub_g, watt, wo, fg,
                       lam_init, final, tm=512).reshape(bp, s, D_MODEL)
        outs[0].append(k.reshape(bp, s, N_HEADS, V_HEAD_DIM))
        outs[1].append(v.reshape(bp, s, N_HEADS, V_HEAD_DIM))
        outs[2].append(c_n)
        outs[3].append(pl_n)

        xs_rows = xs.reshape(1, bd * t, D_MODEL)
        q, k, kb, v, vb, ag, sg = _qkv_proj(xs_rows, ng, w_qkv, cos_s, sin_s, tm=bd * t)
        mcp, c_n, pl_n = _conv_pool(xs, ng, w_cp, state_conv[l], state_pool[l], p, tm=t, pos_base=past)
        per_seq = lambda a: a.reshape(bd, t * N_HEADS, V_HEAD_DIM)
        o = _sample_attention(page_table, lam_p, per_seq(q), per_seq(k), per_seq(v), cache_k, cache_v, l, lam_init)
        flat = lambda a: a.reshape(bd * t, D_MODEL)
        xs = _out_proj(flat(o), flat(ag), flat(sg), flat(mcp), flat(xs), sub_g, watt, wo, fg,
                       lam_init, final, tm=bd * t).reshape(bd, t, D_MODEL)
        outs[4].append(k.reshape(bd, t, N_HEADS, V_HEAD_DIM))
        outs[5].append(v.reshape(bd, t, N_HEADS, V_HEAD_DIM))
        outs[6].append(c_n)
        outs[7].append(pl_n)

    return (xp, xs) + tuple(jnp.stack(o) for o in outs)
```

```python
import functools
import math

import jax
import jax.numpy as jnp
from jax import lax
from jax.experimental import pallas as pl
from jax.experimental.pallas import tpu as pltpu

F32 = jnp.float32
BF16 = jnp.bfloat16

D_MODEL = 1024
W_CONV = 512
W_ATT = 1024
W_POOL = 512
N_HEADS = 8
HEAD_DIM = 64
V_HEAD_DIM = 128
ROPE_THETA = 10000.0
CONV_WIDTH = 31
CONV_BUF = CONV_WIDTH - 1
POOL_WINDOWS = (2, 4, 8, 16)
POOL_GROUP = 128
POOL_MAX = 16
POOL_BUF = POOL_MAX - 1
EPS = 1e-6
LANES = 128
SUBLANES = 8
CONV_HALO = 32
POOL_HALO = 16
VMEM_LIMIT = 56 * 1024 * 1024
LOG2_E = math.log2(math.e)
CONV_ROWS = 32
PAGES_PER_STEP = 8

_C_VAL, _C_GLU, _C_GATE = 0, 512, 1024
_Q, _K, _V, _A_GATE = 1536, 2560, 3584, 4608
_PL_IN, _PL_GATE = 5632, 6144
_G_CONV, _G_ATT, _G_POOL = 6656, 7680, 8704


def _dot(a, b):
    return jnp.dot(a, b, preferred_element_type=F32)


def _dot_nt(a, b):
    return lax.dot_general(a, b, (((1,), (1,)), ((), ())), preferred_element_type=F32)


def _sigmoid(x):
    return 1.0 / (1.0 + jnp.exp(-x))


def _silu(x):
    return x * _sigmoid(x)


def _rms_rows(x, g):
    return x * lax.rsqrt(jnp.mean(x * x, axis=-1, keepdims=True) + EPS) * g


def _const_spec(shape):
    nd = len(shape)
    return pl.BlockSpec(shape, lambda *_: (0,) * nd, pipeline_mode=pl.Buffered(1))


def _params(*sem):
    return pltpu.CompilerParams(dimension_semantics=sem, vmem_limit_bytes=VMEM_LIMIT)


def _qkv_kernel(x_ref, g_ref, w_ref, cos_ref, sin_ref,
                q_ref, k_ref, kb_ref, v_ref, vb_ref, ag_ref, sg_ref):
    h = _rms_rows(x_ref[...], g_ref[...]).astype(BF16)
    cos = cos_ref[...]
    sin = sin_ref[...]
    first_half = (lax.broadcasted_iota(jnp.int32, cos.shape, 1) % HEAD_DIM) < (HEAD_DIM // 2)

    def rope(z):
        partner = jnp.where(first_half, pltpu.roll(z, LANES - HEAD_DIM // 2, 1),
                            pltpu.roll(z, HEAD_DIM // 2, 1))
        return z * cos + partner * sin

    scale = HEAD_DIM ** -0.5 * LOG2_E
    zq = _dot(h, w_ref[:, 0:W_ATT])
    for c in range(W_ATT // LANES):
        sl = slice(c * LANES, (c + 1) * LANES)
        q_ref[:, sl] = (rope(zq[:, sl]) * scale).astype(BF16)
    zk = _dot(h, w_ref[:, W_ATT:2 * W_ATT])
    for c in range(W_ATT // LANES):
        sl = slice(c * LANES, (c + 1) * LANES)
        kr = rope(zk[:, sl])
        k_ref[:, sl] = kr
        kb_ref[:, sl] = kr.astype(BF16)
    zv = _dot(h, w_ref[:, 2 * W_ATT:3 * W_ATT])
    v_ref[...] = zv
    vb_ref[...] = zv.astype(BF16)
    ag_ref[...] = _silu(_dot(h, w_ref[:, 3 * W_ATT:4 * W_ATT])).astype(BF16)
    sg_ref[...] = _sigmoid(_dot(h, w_ref[:, 4 * W_ATT:5 * W_ATT])).astype(BF16)


def _qkv_proj(x, norm_g, w_qkv, cos_tab, sin_tab, tm):
    nb, n, _ = x.shape
    row = lambda b, i: (b, i, 0)
    tab = lambda b, i: (i, 0)
    blk = pl.BlockSpec((None, tm, D_MODEL), row)
    shp = lambda dt: jax.ShapeDtypeStruct((nb, n, W_ATT), dt)
    return pl.pallas_call(
        _qkv_kernel,
        grid=(nb, n // tm),
        in_specs=[blk, _const_spec((1, D_MODEL)), _const_spec(w_qkv.shape),
                  pl.BlockSpec((tm, LANES), tab), pl.BlockSpec((tm, LANES), tab)],
        out_specs=[blk] * 7,
        out_shape=[shp(BF16), shp(F32), shp(BF16), shp(F32), shp(BF16), shp(BF16), shp(BF16)],
        compiler_params=_params("parallel", "parallel"),
        name="qkv_proj",
    )(x, norm_g, w_qkv, cos_tab, sin_tab)


def _conv_pool_kernel(x_ref, g_ref, w_ref, cinit_ref, pinit_ref,
                      cw_ref, cb_ref, lng_ref, lnb_ref, pw_ref, cup_ref,
                      poolw_ref, poolb_ref, pscale_ref, pup_ref,
                      mcp_ref, cstate_ref, pstate_ref, uext, pext, ushift, *, tm, nt, pos_base):
    t = pl.program_id(1)

    @pl.when(t == 0)
    def _():
        uext[0:CONV_HALO - CONV_BUF, :] = jnp.zeros((CONV_HALO - CONV_BUF, W_CONV), F32)
        uext[CONV_HALO - CONV_BUF:CONV_HALO, :] = cinit_ref[...]
        pext[0:POOL_HALO - POOL_BUF, :] = jnp.zeros((POOL_HALO - POOL_BUF, W_POOL), F32)
        pext[POOL_HALO - POOL_BUF:POOL_HALO, :] = pinit_ref[...]

    h = _rms_rows(x_ref[...], g_ref[...]).astype(BF16)

    def proj(off, width):
        return _dot(h, w_ref[:, off:off + width])

    u = proj(0, W_CONV) * _sigmoid(proj(W_CONV, W_CONV))
    uext[CONV_HALO:CONV_HALO + tm, :] = u
    first = CONV_HALO - CONV_BUF
    n_shift = tm + CONV_HALO - SUBLANES
    for b in range(1, SUBLANES):
        ushift[b - 1, 0:n_shift, :] = uext[b:b + n_shift, :]
    rc = min(CONV_ROWS, tm)
    chunks = []
    for r in range(0, tm, rc):
        acc = jnp.zeros((rc, W_CONV), F32) + cb_ref[...]
        for j in range(CONV_WIDTH):
            b = (first + j) % SUBLANES
            at = first + j - b + r
            win = uext[at:at + rc, :] if b == 0 else ushift[b - 1, at:at + rc, :]
            acc = acc + win * cw_ref[j:j + 1, :]
        xc = acc - jnp.mean(acc, axis=-1, keepdims=True)
        c = xc * lax.rsqrt(jnp.mean(xc * xc, axis=-1, keepdims=True) + EPS) * lng_ref[...] + lnb_ref[...]
        chunks.append(_silu(c).astype(BF16))
    c = jnp.concatenate(chunks, axis=0)
    c = _dot(c, pw_ref[...]) * _silu(proj(2 * W_CONV, W_CONV))
    br_conv = _dot(c.astype(BF16), cup_ref[...])

    off_pool = 3 * W_CONV
    pl_in = proj(off_pool, W_POOL)
    pext[POOL_HALO:POOL_HALO + tm, :] = pl_in
    pos = pos_base + t * tm + lax.broadcasted_iota(jnp.int32, (tm, POOL_GROUP), 0)
    pooled = []
    for g, w in enumerate(POOL_WINDOWS):
        cols = slice(g * POOL_GROUP, (g + 1) * POOL_GROUP)
        tok = pext[POOL_HALO:POOL_HALO + tm, cols]
        win = tok
        for i in range(1, w):
            win = win + pext[POOL_HALO - i:POOL_HALO - i + tm, cols]
        cnt = jnp.minimum(w, pos + 1).astype(F32)
        m = (win / cnt - tok).astype(BF16)
        pooled.append(_dot(m, poolw_ref[g]))
    m = jnp.concatenate(pooled, axis=-1) + poolb_ref[...]
    m = m * pscale_ref[...] * _silu(proj(off_pool + W_POOL, W_POOL))
    br_pool = _dot(m.astype(BF16), pup_ref[...])

    off_gate = off_pool + 2 * W_POOL
    mcp_ref[...] = (_sigmoid(proj(off_gate, D_MODEL)) * br_conv
                    + _sigmoid(proj(off_gate + D_MODEL, D_MODEL)) * br_pool)

    @pl.when(t == nt - 1)
    def _():
        cstate_ref[...] = uext[tm + CONV_HALO - CONV_BUF:tm + CONV_HALO, :]
        pstate_ref[...] = pext[tm + POOL_HALO - POOL_BUF:tm + POOL_HALO, :]

    if nt > 1:
        uext[0:CONV_HALO, :] = uext[tm:tm + CONV_HALO, :]
        pext[0:POOL_HALO, :] = pext[tm:tm + POOL_HALO, :]


def _conv_pool(x, norm_g, w_cp, conv_init, pool_init, p, tm, pos_base):
    nb, n, _ = x.shape
    nt = n // tm
    row = lambda b, i: (b, i, 0)
    per_b = lambda b, i: (b, 0, 0)
    consts = [p['conv_w'], p['conv_b'], p['conv_ln_g'], p['conv_ln_b'], p['conv_pw'], p['w_conv_up'],
              p['pool_w'], p['pool_b'], p['pool_scale'], p['w_pool_up']]
    return pl.pallas_call(
        functools.partial(_conv_pool_kernel, tm=tm, nt=nt, pos_base=pos_base),
        grid=(nb, nt),
        in_specs=[pl.BlockSpec((None, tm, D_MODEL), row), _const_spec((1, D_MODEL)), _const_spec(w_cp.shape),
                  pl.BlockSpec((None, CONV_BUF, W_CONV), per_b), pl.BlockSpec((None, POOL_BUF, W_POOL), per_b)]
                 + [_const_spec(c.shape) for c in consts],
        out_specs=[pl.BlockSpec((None, tm, D_MODEL), row),
                   pl.BlockSpec((None, CONV_BUF, W_CONV), per_b), pl.BlockSpec((None, POOL_BUF, W_POOL), per_b)],
        out_shape=[jax.ShapeDtypeStruct((nb, n, D_MODEL), F32),
                   jax.ShapeDtypeStruct((nb, CONV_BUF, W_CONV), F32),
                   jax.ShapeDtypeStruct((nb, POOL_BUF, W_POOL), F32)],
        scratch_shapes=[pltpu.VMEM((CONV_HALO + tm, W_CONV), F32), pltpu.VMEM((POOL_HALO + tm, W_POOL), F32),
                        pltpu.VMEM((SUBLANES - 1, CONV_HALO + tm - SUBLANES, W_CONV), F32)],
        compiler_params=_params("parallel", "arbitrary"),
        name="conv_pool",
    )(x, norm_g, w_cp, conv_init, pool_init, *consts)


def _lambda(lam_ref, lam_init):
    lp = lam_ref[...]
    return (jnp.exp(jnp.sum(lp[0:1] * lp[1:2], axis=-1, keepdims=True))
            - jnp.exp(jnp.sum(lp[2:3] * lp[3:4], axis=-1, keepdims=True)) + lam_init)


def _split_branches(q):
    lane = lax.broadcasted_iota(jnp.int32, q.shape, 1)
    zero = jnp.zeros_like(q)
    return jnp.concatenate([jnp.where(lane < HEAD_DIM, q, zero), jnp.where(lane >= HEAD_DIM, q, zero)], axis=0)


def _tree(op, xs):
    while len(xs) > 1:
        xs = [op(xs[i], xs[i + 1]) if i + 1 < len(xs) else xs[i] for i in range(0, len(xs), 2)]
    return xs[0]


def _online_softmax_step(scores, values, m_scr, l_scr, acc_scr):
    chunks = [s[:, c:c + LANES] for s in scores for c in range(0, s.shape[1], LANES)]
    m_prev = m_scr[...]
    m_new = jnp.maximum(m_prev, jnp.max(_tree(jnp.maximum, chunks), axis=-1, keepdims=True))
    alpha = jnp.exp2(m_prev - m_new)
    probs = [jnp.exp2(ch - m_new) for ch in chunks]
    l_scr[...] = alpha * l_scr[...] + _tree(jnp.add, probs)
    pv, at = None, 0
    for s, v in zip(scores, values):
        n = s.shape[1] // LANES
        p = jnp.concatenate([x.astype(BF16) for x in probs[at:at + n]], axis=1)
        at += n
        pv = _dot(p, v) if pv is None else pv + _dot(p, v)
    acc_scr[...] = alpha * acc_scr[...] + pv
    m_scr[...] = m_new


def _init_softmax(m_scr, l_scr, acc_scr):
    m_scr[...] = jnp.full(m_scr.shape, -jnp.inf, F32)
    l_scr[...] = jnp.zeros(l_scr.shape, F32)
    acc_scr[...] = jnp.zeros(acc_scr.shape, F32)


def _finish_softmax(lam_ref, lam_init, l_scr, acc_scr):
    o = acc_scr[...] / jnp.sum(l_scr[...], axis=-1, keepdims=True)
    half = o.shape[0] // 2
    return o[:half] - _lambda(lam_ref, lam_init) * o[half:]


def _prompt_attn_kernel(lam_ref, q_ref, k_ref, v_ref, o_ref, sa_scr, sb_scr, m_scr, l_scr, acc_scr,
                        *, tq, lam_init):
    qi = pl.program_id(2)
    q2 = _split_branches(q_ref[...])
    _init_softmax(m_scr, l_scr, acc_scr)

    def rows_of(j):
        return pl.ds(pl.multiple_of(j * tq, tq), tq)

    def scores(j):
        return _dot_nt(q2, k_ref[rows_of(j), :])

    def update(s_scr, j, diagonal):
        s = s_scr[...]
        if diagonal:
            qpos = lax.broadcasted_iota(jnp.int32, s.shape, 0) % tq
            kpos = lax.broadcasted_iota(jnp.int32, s.shape, 1)
            s = jnp.where(kpos <= qpos, s, -jnp.inf)
        _online_softmax_step([s], [v_ref[rows_of(j), :]], m_scr, l_scr, acc_scr)

    sa_scr[...] = scores(0)

    def two_blocks(jj, carry):
        j = 2 * jj
        sb_scr[...] = scores(j + 1)
        update(sa_scr, j, False)
        sa_scr[...] = scores(j + 2)
        update(sb_scr, j + 1, False)
        return carry

    lax.fori_loop(0, qi // 2, two_blocks, 0)

    @pl.when(qi % 2 == 0)
    def _():
        update(sa_scr, qi, True)

    @pl.when(qi % 2 == 1)
    def _():
        sb_scr[...] = scores(qi)
        update(sa_scr, qi - 1, False)
        update(sb_scr, qi, True)

    o_ref[...] = _finish_softmax(lam_ref, lam_init, l_scr, acc_scr)


def _prompt_attention(lam_p, q, k, v, lam_init, tq):
    nb, s, _ = q.shape
    return pl.pallas_call(
        functools.partial(_prompt_attn_kernel, tq=tq, lam_init=lam_init),
        grid=(nb, N_HEADS, s // tq),
        in_specs=[_const_spec(lam_p.shape),
                  pl.BlockSpec((None, tq, V_HEAD_DIM), lambda b, h, i: (b, i, h)),
                  pl.BlockSpec((None, s, V_HEAD_DIM), lambda b, h, i: (b, 0, h)),
                  pl.BlockSpec((None, s, V_HEAD_DIM), lambda b, h, i: (b, 0, h))],
        out_specs=pl.BlockSpec((None, tq, V_HEAD_DIM), lambda b, h, i: (b, i, h)),
        out_shape=jax.ShapeDtypeStruct((nb, s, W_ATT), F32),
        scratch_shapes=[pltpu.VMEM((2 * tq, tq), F32), pltpu.VMEM((2 * tq, tq), F32),
                        pltpu.VMEM((2 * tq, LANES), F32), pltpu.VMEM((2 * tq, LANES), F32),
                        pltpu.VMEM((2 * tq, V_HEAD_DIM), F32)],
        compiler_params=_params("parallel", "parallel", "arbitrary"),
        name="prompt_attention",
    )(lam_p, q, k, v)


def _sample_attn_kernel(pt_ref, lam_ref, bias_ref, bias_new_ref, q_ref, kn_ref, vn_ref, *refs,
                        n_steps, n_group, lam_init):
    kc_refs, vc_refs = refs[:n_group], refs[n_group:2 * n_group]
    o_ref, m_scr, l_scr, acc_scr = refs[2 * n_group:]
    j = pl.program_id(1)
    q2 = _split_branches(q_ref[...])

    @pl.when(j == 0)
    def _():
        _init_softmax(m_scr, l_scr, acc_scr)

    def flat(ref):
        page, heads, d = ref.shape
        return ref[...].reshape(page * heads, d).astype(BF16)

    bias = bias_ref[...]
    scores = [_dot_nt(q2, flat(kc)) + bias for kc in kc_refs]
    _online_softmax_step(scores, [flat(vc) for vc in vc_refs], m_scr, l_scr, acc_scr)

    @pl.when(j == n_steps - 1)
    def _():
        s_new = _dot_nt(q2, kn_ref[...].astype(BF16)) + bias_new_ref[...]
        _online_softmax_step([s_new], [vn_ref[...].astype(BF16)], m_scr, l_scr, acc_scr)
        o_ref[...] = _finish_softmax(lam_ref, lam_init, l_scr, acc_scr)


def _head_bias(n_rows, n_cols, t_new, n_valid):
    r = jnp.arange(n_rows)[:, None]
    c = jnp.arange(n_cols)[None, :]
    ok = (r % N_HEADS == c % N_HEADS) & (c < n_valid)
    if t_new is not None:
        ok &= c // N_HEADS <= (r // N_HEADS) % t_new
    return jnp.where(ok, 0.0, -jnp.inf).astype(F32)


def _sample_attention(page_table, lam_p, q, k_new, v_new, cache_k, cache_v, layer, lam_init):
    nb, n_pages = page_table.shape
    page = cache_k.shape[2]
    n_new = q.shape[1]
    t_new = n_new // N_HEADS
    n_group = PAGES_PER_STEP
    n_steps = n_pages // n_group
    rows = 2 * n_new
    pad = ((0, 0), (0, LANES - n_new), (0, 0))
    k_new, v_new = jnp.pad(k_new, pad), jnp.pad(v_new, pad)
    bias = _head_bias(rows, page * N_HEADS, None, page * N_HEADS)
    bias_new = _head_bias(rows, LANES, t_new, n_new)
    per_b = lambda b, j, pt: (b, 0, 0)
    const = lambda b, j, pt: (0, 0)

    def page_spec(g):
        return pl.BlockSpec((None, None, page, N_HEADS, V_HEAD_DIM),
                            lambda b, j, pt: (layer, pt[b, j * n_group + g], 0, 0, 0))

    new_spec = pl.BlockSpec((None, LANES, V_HEAD_DIM), per_b)
    pages = [page_spec(g) for g in range(n_group)]
    return pl.pallas_call(
        functools.partial(_sample_attn_kernel, n_steps=n_steps, n_group=n_group, lam_init=lam_init),
        grid_spec=pltpu.PrefetchScalarGridSpec(
            num_scalar_prefetch=1,
            grid=(nb, n_steps),
            in_specs=[pl.BlockSpec(lam_p.shape, const), pl.BlockSpec(bias.shape, const),
                      pl.BlockSpec(bias_new.shape, const),
                      pl.BlockSpec((None, n_new, V_HEAD_DIM), per_b), new_spec, new_spec] + pages + pages,
            out_specs=pl.BlockSpec((None, n_new, V_HEAD_DIM), per_b),
            scratch_shapes=[pltpu.VMEM((rows, LANES), F32), pltpu.VMEM((rows, LANES), F32),
                            pltpu.VMEM((rows, V_HEAD_DIM), F32)]),
        out_shape=jax.ShapeDtypeStruct((nb, n_new, V_HEAD_DIM), F32),
        compiler_params=_params("parallel", "arbitrary"),
        name="sample_attention",
    )(page_table, lam_p, bias, bias_new, q, k_new, v_new, *([cache_k] * n_group), *([cache_v] * n_group))


def _out_kernel(o_ref, ag_ref, sg_ref, mcp_ref, x_ref, subg_ref, watt_ref, wout_ref, fg_ref, y_ref,
                *, lam_init, final):
    heads = []
    for hd in range(N_HEADS):
        oh = o_ref[:, hd * V_HEAD_DIM:(hd + 1) * V_HEAD_DIM]
        heads.append(_rms_rows(oh, subg_ref[...]) * (1.0 - lam_init))
    o = jnp.concatenate(heads, axis=-1) * ag_ref[...].astype(F32)
    br_att = _dot(o.astype(BF16), watt_ref[...])
    merged = sg_ref[...].astype(F32) * br_att + mcp_ref[...]
    y = x_ref[...] + _dot(merged.astype(BF16), wout_ref[...])
    y_ref[...] = _rms_rows(y, fg_ref[...]) if final else y


def _out_proj(o, ag, sg, mcp, x, subln_g, w_att_up, w_out, final_g, lam_init, final, tm):
    n = x.shape[0]
    row = lambda i: (i, 0)
    blk = pl.BlockSpec((tm, D_MODEL), row)
    return pl.pallas_call(
        functools.partial(_out_kernel, lam_init=lam_init, final=final),
        grid=(n // tm,),
        in_specs=[blk] * 5 + [_const_spec(subln_g.shape), _const_spec(w_att_up.shape),
                              _const_spec(w_out.shape), _const_spec(final_g.shape)],
        out_specs=blk,
        out_shape=jax.ShapeDtypeStruct((n, D_MODEL), F32),
        compiler_params=_params("parallel"),
        name="out_proj",
    )(o, ag, sg, mcp, x, subln_g, w_att_up, w_out, final_g)


def _rope_tables(pos):
    half = HEAD_DIM // 2
    inv = ROPE_THETA ** (-jnp.arange(half, dtype=F32) / half)
    ang = pos.astype(F32)[:, None] * inv[None, :]
    cos, sin = jnp.cos(ang), jnp.sin(ang)
    reps = LANES // HEAD_DIM
    return (jnp.tile(jnp.concatenate([cos, cos], axis=-1), (1, reps)),
            jnp.tile(jnp.concatenate([-sin, sin], axis=-1), (1, reps)))


def kernel(x_prompt, x_sample, cache_k, cache_v, state_conv, state_pool, page_table, norm_g, w_in, conv_w, conv_b,
           conv_ln_g, conv_ln_b, conv_pw, lambda_q1, lambda_k1, lambda_q2, lambda_k2, subln_g, pool_w, pool_b,
           pool_scale, w_conv_up, w_att_up, w_pool_up, w_out, final_g):
    bp, s, _ = x_prompt.shape
    bd, t, _ = x_sample.shape
    depth = w_in.shape[0]
    past = page_table.shape[1] * cache_k.shape[2]
    cos_p, sin_p = _rope_tables(jnp.arange(s, dtype=jnp.int32))
    cos_s, sin_s = _rope_tables(jnp.tile(past + jnp.arange(t, dtype=jnp.int32), bd))
    conv0 = jnp.zeros((bp, CONV_BUF, W_CONV), F32)
    pool0 = jnp.zeros((bp, POOL_BUF, W_POOL), F32)
    fg = final_g.reshape(1, D_MODEL)

    xp, xs = x_prompt, x_sample
    outs = [[] for _ in range(8)]
    for l in range(depth):
        lam_init = 0.8 - 0.6 * math.exp(-0.3 * l)
        final = l == depth - 1
        wl = w_in[l]
        w_qkv = jnp.concatenate([wl[:, _Q:_PL_IN], wl[:, _G_ATT:_G_POOL]], axis=1).astype(BF16)
        w_cp = jnp.concatenate([wl[:, _C_VAL:_Q], wl[:, _PL_IN:_G_ATT], wl[:, _G_POOL:]], axis=1).astype(BF16)
        ng = norm_g[l].reshape(1, D_MODEL)
        lam_p = jnp.stack([lambda_q1[l], lambda_k1[l], lambda_q2[l], lambda_k2[l]])
        p = dict(conv_w=conv_w[l], conv_b=conv_b[l].reshape(1, -1), conv_ln_g=conv_ln_g[l].reshape(1, -1),
                 conv_ln_b=conv_ln_b[l].reshape(1, -1), conv_pw=conv_pw[l].astype(BF16),
                 w_conv_up=w_conv_up[l].astype(BF16), pool_w=pool_w[l].astype(BF16),
                 pool_b=pool_b[l].reshape(1, -1), pool_scale=pool_scale[l].reshape(1, -1),
                 w_pool_up=w_pool_up[l].astype(BF16))
        sub_g = subln_g[l].reshape(1, V_HEAD_DIM)
        watt = w_att_up[l].astype(BF16)
        wo = w_out[l].astype(BF16)

        q, k, kb, v, vb, ag, sg = _qkv_proj(xp, ng, w_qkv, cos_p, sin_p, tm=512)
        mcp, c_n, pl_n = _conv_pool(xp, ng, w_cp, conv0, pool0, p, tm=512, pos_base=0)
        o = _prompt_attention(lam_p, q, kb, vb, lam_init, tq=512)
        flat = lambda a: a.reshape(bp * s, D_MODEL)
        xp = _out_proj(flat(o), flat(ag), flat(sg), flat(mcp), flat(xp), sub_g, watt, wo, fg,
                       lam_init, final, tm=512).reshape(bp, s, D_MODEL)
        outs[0].append(k.reshape(bp, s, N_HEADS, V_HEAD_DIM))
        outs[1].append(v.reshape(bp, s, N_HEADS, V_HEAD_DIM))
        outs[2].append(c_n)
        outs[3].append(pl_n)

        xs_rows = xs.reshape(1, bd * t, D_MODEL)
        q, k, kb, v, vb, ag, sg = _qkv_proj(xs_rows, ng, w_qkv, cos_s, sin_s, tm=bd * t)
        mcp, c_n, pl_n = _conv_pool(xs, ng, w_cp, state_conv[l], state_pool[l], p, tm=t, pos_base=past)
        per_seq = lambda a: a.reshape(bd, t * N_HEADS, V_HEAD_DIM)
        o = _sample_attention(page_table, lam_p, per_seq(q), per_seq(k), per_seq(v), cache_k, cache_v, l, lam_init)
        flat = lambda a: a.reshape(bd * t, D_MODEL)
        xs = _out_proj(flat(o), flat(ag), flat(sg), flat(mcp), flat(xs), sub_g, watt, wo, fg,
                       lam_init, final, tm=bd * t).reshape(bd, t, D_MODEL)
        outs[4].append(k.reshape(bd, t, N_HEADS, V_HEAD_DIM))
        outs[5].append(v.reshape(bd, t, N_HEADS, V_HEAD_DIM))
        outs[6].append(c_n)
        outs[7].append(pl_n)

    return (xp, xs) + tuple(jnp.stack(o) for o in outs)
```

```python
import functools
import math

import jax
import jax.numpy as jnp
from jax import lax
from jax.experimental import pallas as pl
from jax.experimental.pallas import tpu as pltpu

F32 = jnp.float32
BF16 = jnp.bfloat16

D_MODEL = 1024
W_CONV = 512
W_ATT = 1024
W_POOL = 512
N_HEADS = 8
HEAD_DIM = 64
V_HEAD_DIM = 128
ROPE_THETA = 10000.0
CONV_WIDTH = 31
CONV_BUF = CONV_WIDTH - 1
POOL_WINDOWS = (2, 4, 8, 16)
POOL_GROUP = 128
POOL_MAX = 16
POOL_BUF = POOL_MAX - 1
EPS = 1e-6
LANES = 128
SUBLANES = 8
CONV_HALO = 32
POOL_HALO = 16
VMEM_LIMIT = 56 * 1024 * 1024
LOG2_E = math.log2(math.e)
CONV_ROWS = 32
PAGES_PER_STEP = 16
ATTN_STRIP = 64

_C_VAL, _C_GLU, _C_GATE = 0, 512, 1024
_Q, _K, _V, _A_GATE = 1536, 2560, 3584, 4608
_PL_IN, _PL_GATE = 5632, 6144
_G_CONV, _G_ATT, _G_POOL = 6656, 7680, 8704


def _dot(a, b):
    return jnp.dot(a, b, preferred_element_type=F32)


def _dot_nt(a, b):
    return lax.dot_general(a, b, (((1,), (1,)), ((), ())), preferred_element_type=F32)


def _sigmoid(x):
    return 1.0 / (1.0 + jnp.exp(-x))


def _silu(x):
    return x * _sigmoid(x)


def _rms_rows(x, g):
    return x * lax.rsqrt(jnp.mean(x * x, axis=-1, keepdims=True) + EPS) * g


def _const_spec(shape):
    nd = len(shape)
    return pl.BlockSpec(shape, lambda *_: (0,) * nd, pipeline_mode=pl.Buffered(1))


def _layer_spec(arr, layer):
    rest = arr.shape[1:]
    return pl.BlockSpec((None,) + rest, lambda *_: (layer,) + (0,) * len(rest), pipeline_mode=pl.Buffered(1))


def _params(*sem):
    return pltpu.CompilerParams(dimension_semantics=sem, vmem_limit_bytes=VMEM_LIMIT)


def _qkv_kernel(x_ref, g_ref, w_ref, cos_ref, sin_ref, *refs, per_head):
    q_ref, k_ref, kb_ref, v_ref, vb_ref, ag_ref, sg_ref = refs[-7:]
    h = _rms_rows(x_ref[...], g_ref[...]).astype(BF16)

    def proj(off):
        return _dot(h, w_ref[:, off:off + W_ATT])

    def store_f32(ref, c, val):
        if per_head:
            ref[:, c, :] = val
        else:
            ref[:, c * LANES:(c + 1) * LANES] = val

    cos = cos_ref[...]
    sin = sin_ref[...]
    first_half = (lax.broadcasted_iota(jnp.int32, cos.shape, 1) % HEAD_DIM) < (HEAD_DIM // 2)

    def rope(z):
        partner = jnp.where(first_half, pltpu.roll(z, LANES - HEAD_DIM // 2, 1),
                            pltpu.roll(z, HEAD_DIM // 2, 1))
        return z * cos + partner * sin

    scale = HEAD_DIM ** -0.5 * LOG2_E
    zq = proj(_Q)
    for c in range(W_ATT // LANES):
        sl = slice(c * LANES, (c + 1) * LANES)
        q_ref[:, sl] = (rope(zq[:, sl]) * scale).astype(BF16)
    zk = proj(_K)
    for c in range(W_ATT // LANES):
        sl = slice(c * LANES, (c + 1) * LANES)
        kr = rope(zk[:, sl])
        store_f32(k_ref, c, kr)
        kb_ref[:, sl] = kr.astype(BF16)
    zv = proj(_V)
    for c in range(W_ATT // LANES):
        store_f32(v_ref, c, zv[:, c * LANES:(c + 1) * LANES])
    vb_ref[...] = zv.astype(BF16)
    ag_ref[...] = _silu(proj(_A_GATE)).astype(BF16)
    sg_ref[...] = _sigmoid(proj(_G_ATT)).astype(BF16)


def _qkv_proj(x, norm_g, w_in, cos_tab, sin_tab, layer, tm, kv_stack=None):
    nb, n, _ = x.shape
    depth = w_in.shape[0]
    row = lambda b, i: (b, i, 0)
    tab = lambda b, i: (i, 0)
    blk = pl.BlockSpec((None, tm, D_MODEL), row)
    shp = lambda dt: jax.ShapeDtypeStruct((nb, n, W_ATT), dt)
    kv_spec, kv_shape, alias_in, aliases = blk, shp(F32), [], {}
    if kv_stack is not None:
        k_all, v_all = kv_stack
        kv_spec = pl.BlockSpec((None, None, tm, N_HEADS, V_HEAD_DIM), lambda b, i: (layer, b, i, 0, 0))
        kv_shape = jax.ShapeDtypeStruct((depth, nb, n, N_HEADS, V_HEAD_DIM), F32)
        if k_all is not None:
            alias_in = [k_all, v_all]
            aliases = {5: 1, 6: 3}
    return pl.pallas_call(
        functools.partial(_qkv_kernel, per_head=kv_stack is not None),
        grid=(nb, n // tm),
        in_specs=[blk, _layer_spec(norm_g, layer), _layer_spec(w_in, layer),
                  pl.BlockSpec((tm, LANES), tab), pl.BlockSpec((tm, LANES), tab)]
                 + [pl.BlockSpec(memory_space=pl.ANY)] * len(alias_in),
        out_specs=[blk, kv_spec, blk, kv_spec, blk, blk, blk],
        out_shape=[shp(BF16), kv_shape, shp(BF16), kv_shape, shp(BF16), shp(BF16), shp(BF16)],
        input_output_aliases=aliases,
        compiler_params=_params("parallel", "parallel"),
        name="qkv_proj",
    )(x, norm_g, w_in, cos_tab, sin_tab, *alias_in)


def _conv_pool_kernel(x_ref, g_ref, w_ref, cinit_ref, pinit_ref,
                      cw_ref, cb_ref, lng_ref, lnb_ref, pw_ref, cup_ref,
                      poolw_ref, poolb_ref, pscale_ref, pup_ref,
                      mcp_ref, cstate_ref, pstate_ref, uext, pext, ushift, *, tm, nt, pos_base):
    t = pl.program_id(1)

    @pl.when(t == 0)
    def _():
        uext[0:CONV_HALO - CONV_BUF, :] = jnp.zeros((CONV_HALO - CONV_BUF, W_CONV), F32)
        uext[CONV_HALO - CONV_BUF:CONV_HALO, :] = cinit_ref[...]
        pext[0:POOL_HALO - POOL_BUF, :] = jnp.zeros((POOL_HALO - POOL_BUF, W_POOL), F32)
        pext[POOL_HALO - POOL_BUF:POOL_HALO, :] = pinit_ref[...]

    h = _rms_rows(x_ref[...], g_ref[...]).astype(BF16)

    def proj(off, width):
        return _dot(h, w_ref[:, off:off + width])

    u = proj(_C_VAL, W_CONV) * _sigmoid(proj(_C_GLU, W_CONV))
    uext[CONV_HALO:CONV_HALO + tm, :] = u
    first = CONV_HALO - CONV_BUF
    n_shift = tm + CONV_HALO - SUBLANES
    for b in range(1, SUBLANES):
        ushift[b - 1, 0:n_shift, :] = uext[b:b + n_shift, :]
    rc = min(CONV_ROWS, tm)
    chunks = []
    for r in range(0, tm, rc):
        acc = jnp.zeros((rc, W_CONV), F32) + cb_ref[...]
        for j in range(CONV_WIDTH):
            b = (first + j) % SUBLANES
            at = first + j - b + r
            win = uext[at:at + rc, :] if b == 0 else ushift[b - 1, at:at + rc, :]
            acc = acc + win * cw_ref[j:j + 1, :]
        xc = acc - jnp.mean(acc, axis=-1, keepdims=True)
        c = xc * lax.rsqrt(jnp.mean(xc * xc, axis=-1, keepdims=True) + EPS) * lng_ref[...] + lnb_ref[...]
        chunks.append(_silu(c).astype(BF16))
    c = jnp.concatenate(chunks, axis=0)
    c = _dot(c, pw_ref[...]) * _silu(proj(_C_GATE, W_CONV))
    br_conv = _dot(c.astype(BF16), cup_ref[...])

    pl_in = proj(_PL_IN, W_POOL)
    pext[POOL_HALO:POOL_HALO + tm, :] = pl_in
    pos = pos_base + t * tm + lax.broadcasted_iota(jnp.int32, (tm, POOL_GROUP), 0)
    pooled = []
    for g, w in enumerate(POOL_WINDOWS):
        cols = slice(g * POOL_GROUP, (g + 1) * POOL_GROUP)
        tok = pext[POOL_HALO:POOL_HALO + tm, cols]
        win = tok
        for i in range(1, w):
            win = win + pext[POOL_HALO - i:POOL_HALO - i + tm, cols]
        cnt = jnp.minimum(w, pos + 1).astype(F32)
        m = (win / cnt - tok).astype(BF16)
        pooled.append(_dot(m, poolw_ref[g]))
    m = jnp.concatenate(pooled, axis=-1) + poolb_ref[...]
    m = m * pscale_ref[...] * _silu(proj(_PL_GATE, W_POOL))
    br_pool = _dot(m.astype(BF16), pup_ref[...])

    mcp_ref[...] = (_sigmoid(proj(_G_CONV, D_MODEL)) * br_conv
                    + _sigmoid(proj(_G_POOL, D_MODEL)) * br_pool)

    @pl.when(t == nt - 1)
    def _():
        cstate_ref[...] = uext[tm + CONV_HALO - CONV_BUF:tm + CONV_HALO, :]
        pstate_ref[...] = pext[tm + POOL_HALO - POOL_BUF:tm + POOL_HALO, :]

    if nt > 1:
        uext[0:CONV_HALO, :] = uext[tm:tm + CONV_HALO, :]
        pext[0:POOL_HALO, :] = pext[tm:tm + POOL_HALO, :]


def _conv_pool(x, norm_g, w_in, conv_init, pool_init, p, layer, tm, pos_base):
    nb, n, _ = x.shape
    nt = n // tm
    row = lambda b, i: (b, i, 0)
    per_b = lambda b, i: (b, 0, 0)
    consts = [p['conv_w'], p['conv_b'], p['conv_ln_g'], p['conv_ln_b'], p['conv_pw'], p['w_conv_up'],
              p['pool_w'], p['pool_b'], p['pool_scale'], p['w_pool_up']]
    return pl.pallas_call(
        functools.partial(_conv_pool_kernel, tm=tm, nt=nt, pos_base=pos_base),
        grid=(nb, nt),
        in_specs=[pl.BlockSpec((None, tm, D_MODEL), row), _layer_spec(norm_g, layer), _layer_spec(w_in, layer),
                  pl.BlockSpec((None, CONV_BUF, W_CONV), per_b), pl.BlockSpec((None, POOL_BUF, W_POOL), per_b)]
                 + [_layer_spec(c, layer) for c in consts],
        out_specs=[pl.BlockSpec((None, tm, D_MODEL), row),
                   pl.BlockSpec((None, CONV_BUF, W_CONV), per_b), pl.BlockSpec((None, POOL_BUF, W_POOL), per_b)],
        out_shape=[jax.ShapeDtypeStruct((nb, n, D_MODEL), F32),
                   jax.ShapeDtypeStruct((nb, CONV_BUF, W_CONV), F32),
                   jax.ShapeDtypeStruct((nb, POOL_BUF, W_POOL), F32)],
        scratch_shapes=[pltpu.VMEM((CONV_HALO + tm, W_CONV), F32), pltpu.VMEM((POOL_HALO + tm, W_POOL), F32),
                        pltpu.VMEM((SUBLANES - 1, CONV_HALO + tm - SUBLANES, W_CONV), F32)],
        compiler_params=_params("parallel", "arbitrary"),
        name="conv_pool",
    )(x, norm_g, w_in, conv_init, pool_init, *consts)


def _lambda(lam_ref, lam_init):
    lp = lam_ref[...]
    return (jnp.exp(jnp.sum(lp[0:1] * lp[1:2], axis=-1, keepdims=True))
            - jnp.exp(jnp.sum(lp[2:3] * lp[3:4], axis=-1, keepdims=True)) + lam_init)


def _split_branches(q):
    lane = lax.broadcasted_iota(jnp.int32, q.shape, 1)
    zero = jnp.zeros_like(q)
    return jnp.concatenate([jnp.where(lane < HEAD_DIM, q, zero), jnp.where(lane >= HEAD_DIM, q, zero)], axis=0)


def _tree(op, xs):
    while len(xs) > 1:
        xs = [op(xs[i], xs[i + 1]) if i + 1 < len(xs) else xs[i] for i in range(0, len(xs), 2)]
    return xs[0]


def _online_softmax_step(scores, values, m_scr, l_scr, acc_scr):
    chunks = [s[:, c:c + LANES] for s in scores for c in range(0, s.shape[1], LANES)]
    m_prev = m_scr[...]
    m_new = jnp.maximum(m_prev, jnp.max(_tree(jnp.maximum, chunks), axis=-1, keepdims=True))
    alpha = jnp.exp2(m_prev - m_new)
    probs = [jnp.exp2(ch - m_new) for ch in chunks]
    l_scr[...] = alpha * l_scr[...] + _tree(jnp.add, probs)
    pv, at = None, 0
    for s, v in zip(scores, values):
        n = s.shape[1] // LANES
        p = jnp.concatenate([x.astype(BF16) for x in probs[at:at + n]], axis=1)
        at += n
        pv = _dot(p, v) if pv is None else pv + _dot(p, v)
    acc_scr[...] = alpha * acc_scr[...] + pv
    m_scr[...] = m_new


def _init_softmax(m_scr, l_scr, acc_scr):
    m_scr[...] = jnp.full(m_scr.shape, -jnp.inf, F32)
    l_scr[...] = jnp.zeros(l_scr.shape, F32)
    acc_scr[...] = jnp.zeros(acc_scr.shape, F32)


def _finish_softmax(lam_ref, lam_init, l_scr, acc_scr):
    o = acc_scr[...] / jnp.sum(l_scr[...], axis=-1, keepdims=True)
    half = o.shape[0] // 2
    return o[:half] - _lambda(lam_ref, lam_init) * o[half:]


def _prompt_attn_kernel(lam_ref, q_ref, k_ref, v_ref, o_ref, sa_scr, sb_scr, pa_scr, pb_scr, m_scr, l_scr,
                        acc_scr, *, tq, lam_init):
    qi = pl.program_id(2)
    q2 = _split_branches(q_ref[...])
    _init_softmax(m_scr, l_scr, acc_scr)

    def rows_of(j):
        return pl.ds(pl.multiple_of(j * tq, tq), tq)

    def scores(j):
        return _dot_nt(q2, k_ref[rows_of(j), :])

    def update(s_scr, p_scr, j, diagonal):
        for r0 in range(0, 2 * tq, ATTN_STRIP):
            rows = slice(r0, r0 + ATTN_STRIP)
            s = s_scr[rows, :]
            if diagonal:
                qpos = r0 % tq + lax.broadcasted_iota(jnp.int32, s.shape, 0)
                kpos = lax.broadcasted_iota(jnp.int32, s.shape, 1)
                s = jnp.where(kpos <= qpos, s, -jnp.inf)
            chunks = [s[:, c:c + LANES] for c in range(0, tq, LANES)]
            m_prev = m_scr[rows, :]
            m_new = jnp.maximum(m_prev, jnp.max(_tree(jnp.maximum, chunks), axis=-1, keepdims=True))
            alpha = jnp.exp2(m_prev - m_new)
            probs = [jnp.exp2(ch - m_new) for ch in chunks]
            l_scr[rows, :] = alpha * l_scr[rows, :] + _tree(jnp.add, probs)
            acc_scr[rows, :] = alpha * acc_scr[rows, :]
            m_scr[rows, :] = m_new
            p_scr[rows, :] = jnp.concatenate([x.astype(BF16) for x in probs], axis=1)
        acc_scr[...] += _dot(p_scr[...], v_ref[rows_of(j), :])

    sa_scr[...] = scores(0)

    def two_blocks(jj, carry):
        j = 2 * jj
        sb_scr[...] = scores(j + 1)
        update(sa_scr, pa_scr, j, False)
        sa_scr[...] = scores(j + 2)
        update(sb_scr, pb_scr, j + 1, False)
        return carry

    lax.fori_loop(0, qi // 2, two_blocks, 0)

    @pl.when(qi % 2 == 0)
    def _():
        update(sa_scr, pa_scr, qi, True)

    @pl.when(qi % 2 == 1)
    def _():
        sb_scr[...] = scores(qi)
        update(sa_scr, pa_scr, qi - 1, False)
        update(sb_scr, pb_scr, qi, True)

    o_ref[...] = _finish_softmax(lam_ref, lam_init, l_scr, acc_scr)


def _prompt_attention(lam_p, q, k, v, lam_init, tq):
    nb, s, _ = q.shape
    return pl.pallas_call(
        functools.partial(_prompt_attn_kernel, tq=tq, lam_init=lam_init),
        grid=(nb, N_HEADS, s // tq),
        in_specs=[_const_spec(lam_p.shape),
                  pl.BlockSpec((None, tq, V_HEAD_DIM), lambda b, h, i: (b, i, h)),
                  pl.BlockSpec((None, s, V_HEAD_DIM), lambda b, h, i: (b, 0, h)),
                  pl.BlockSpec((None, s, V_HEAD_DIM), lambda b, h, i: (b, 0, h))],
        out_specs=pl.BlockSpec((None, tq, V_HEAD_DIM), lambda b, h, i: (b, i, h)),
        out_shape=jax.ShapeDtypeStruct((nb, s, W_ATT), F32),
        scratch_shapes=[pltpu.VMEM((2 * tq, tq), F32), pltpu.VMEM((2 * tq, tq), F32),
                        pltpu.VMEM((2 * tq, tq), BF16), pltpu.VMEM((2 * tq, tq), BF16),
                        pltpu.VMEM((2 * tq, LANES), F32), pltpu.VMEM((2 * tq, LANES), F32),
                        pltpu.VMEM((2 * tq, V_HEAD_DIM), F32)],
        compiler_params=_params("parallel", "parallel", "arbitrary"),
        name="prompt_attention",
    )(lam_p, q, k, v)


def _sample_attn_kernel(pt_ref, lam_ref, bias_ref, bias_new_ref, q_ref, kn_ref, vn_ref, *refs,
                        n_steps, n_group, lam_init):
    kc_refs, vc_refs = refs[:n_group], refs[n_group:2 * n_group]
    o_ref, m_scr, l_scr, acc_scr = refs[2 * n_group:]
    j = pl.program_id(1)
    q2 = _split_branches(q_ref[...])

    @pl.when(j == 0)
    def _():
        _init_softmax(m_scr, l_scr, acc_scr)

    def flat(ref):
        page, heads, d = ref.shape
        return ref[...].reshape(page * heads, d).astype(BF16)

    bias = bias_ref[...]
    scores = [_dot_nt(q2, flat(kc)) + bias for kc in kc_refs]
    _online_softmax_step(scores, [flat(vc) for vc in vc_refs], m_scr, l_scr, acc_scr)

    @pl.when(j == n_steps - 1)
    def _():
        s_new = _dot_nt(q2, kn_ref[...].astype(BF16)) + bias_new_ref[...]
        _online_softmax_step([s_new], [vn_ref[...].astype(BF16)], m_scr, l_scr, acc_scr)
        o_ref[...] = _finish_softmax(lam_ref, lam_init, l_scr, acc_scr)


def _head_bias(n_rows, n_cols, t_new, n_valid):
    r = jnp.arange(n_rows)[:, None]
    c = jnp.arange(n_cols)[None, :]
    ok = (r % N_HEADS == c % N_HEADS) & (c < n_valid)
    if t_new is not None:
        ok &= c // N_HEADS <= (r // N_HEADS) % t_new
    return jnp.where(ok, 0.0, -jnp.inf).astype(F32)


def _sample_attention(page_table, lam_p, q, k_new, v_new, cache_k, cache_v, layer, lam_init):
    nb, n_pages = page_table.shape
    page = cache_k.shape[2]
    n_new = q.shape[1]
    t_new = n_new // N_HEADS
    n_group = PAGES_PER_STEP
    n_steps = n_pages // n_group
    rows = 2 * n_new
    pad = ((0, 0), (0, LANES - n_new), (0, 0))
    k_new, v_new = jnp.pad(k_new, pad), jnp.pad(v_new, pad)
    bias = _head_bias(rows, page * N_HEADS, None, page * N_HEADS)
    bias_new = _head_bias(rows, LANES, t_new, n_new)
    per_b = lambda b, j, pt: (b, 0, 0)
    const = lambda b, j, pt: (0, 0)

    def page_spec(g):
        return pl.BlockSpec((None, None, page, N_HEADS, V_HEAD_DIM),
                            lambda b, j, pt: (layer, pt[b, j * n_group + g], 0, 0, 0))

    new_spec = pl.BlockSpec((None, LANES, V_HEAD_DIM), per_b)
    pages = [page_spec(g) for g in range(n_group)]
    return pl.pallas_call(
        functools.partial(_sample_attn_kernel, n_steps=n_steps, n_group=n_group, lam_init=lam_init),
        grid_spec=pltpu.PrefetchScalarGridSpec(
            num_scalar_prefetch=1,
            grid=(nb, n_steps),
            in_specs=[pl.BlockSpec(lam_p.shape, const), pl.BlockSpec(bias.shape, const),
                      pl.BlockSpec(bias_new.shape, const),
                      pl.BlockSpec((None, n_new, V_HEAD_DIM), per_b), new_spec, new_spec] + pages + pages,
            out_specs=pl.BlockSpec((None, n_new, V_HEAD_DIM), per_b),
            scratch_shapes=[pltpu.VMEM((rows, LANES), F32), pltpu.VMEM((rows, LANES), F32),
                            pltpu.VMEM((rows, V_HEAD_DIM), F32)]),
        out_shape=jax.ShapeDtypeStruct((nb, n_new, V_HEAD_DIM), F32),
        compiler_params=_params("parallel", "arbitrary"),
        name="sample_attention",
    )(page_table, lam_p, bias, bias_new, q, k_new, v_new, *([cache_k] * n_group), *([cache_v] * n_group))


def _out_kernel(o_ref, ag_ref, sg_ref, mcp_ref, x_ref, subg_ref, watt_ref, wout_ref, fg_ref, y_ref,
                *, lam_init, final):
    heads = []
    for hd in range(N_HEADS):
        oh = o_ref[:, hd * V_HEAD_DIM:(hd + 1) * V_HEAD_DIM]
        heads.append(_rms_rows(oh, subg_ref[...]) * (1.0 - lam_init))
    o = jnp.concatenate(heads, axis=-1) * ag_ref[...].astype(F32)
    br_att = _dot(o.astype(BF16), watt_ref[...])
    merged = sg_ref[...].astype(F32) * br_att + mcp_ref[...]
    y = x_ref[...] + _dot(merged.astype(BF16), wout_ref[...])
    y_ref[...] = _rms_rows(y, fg_ref[...]) if final else y


def _out_proj(o, ag, sg, mcp, x, subln_g, w_att_up, w_out, final_g, layer, lam_init, final, tm):
    n = x.shape[0]
    row = lambda i: (i, 0)
    blk = pl.BlockSpec((tm, D_MODEL), row)
    return pl.pallas_call(
        functools.partial(_out_kernel, lam_init=lam_init, final=final),
        grid=(n // tm,),
        in_specs=[blk] * 5 + [_layer_spec(subln_g, layer), _layer_spec(w_att_up, layer),
                              _layer_spec(w_out, layer), _const_spec(final_g.shape)],
        out_specs=blk,
        out_shape=jax.ShapeDtypeStruct((n, D_MODEL), F32),
        compiler_params=_params("parallel"),
        name="out_proj",
    )(o, ag, sg, mcp, x, subln_g, w_att_up, w_out, final_g)


def _rope_tables(pos):
    half = HEAD_DIM // 2
    inv = ROPE_THETA ** (-jnp.arange(half, dtype=F32) / half)
    ang = pos.astype(F32)[:, None] * inv[None, :]
    cos, sin = jnp.cos(ang), jnp.sin(ang)
    reps = LANES // HEAD_DIM
    return (jnp.tile(jnp.concatenate([cos, cos], axis=-1), (1, reps)),
            jnp.tile(jnp.concatenate([-sin, sin], axis=-1), (1, reps)))


def kernel(x_prompt, x_sample, cache_k, cache_v, state_conv, state_pool, page_table, norm_g, w_in, conv_w, conv_b,
           conv_ln_g, conv_ln_b, conv_pw, lambda_q1, lambda_k1, lambda_q2, lambda_k2, subln_g, pool_w, pool_b,
           pool_scale, w_conv_up, w_att_up, w_pool_up, w_out, final_g):
    bp, s, _ = x_prompt.shape
    bd, t, _ = x_sample.shape
    depth = w_in.shape[0]
    past = page_table.shape[1] * cache_k.shape[2]
    cos_p, sin_p = _rope_tables(jnp.arange(s, dtype=jnp.int32))
    cos_s, sin_s = _rope_tables(jnp.tile(past + jnp.arange(t, dtype=jnp.int32), bd))
    conv0 = jnp.zeros((bp, CONV_BUF, W_CONV), F32)
    pool0 = jnp.zeros((bp, POOL_BUF, W_POOL), F32)
    fg = final_g.reshape(1, D_MODEL)

    rows = lambda a: a.reshape(depth, 1, -1)
    ng, sub_g = rows(norm_g), rows(subln_g)
    w_in_b, watt, wo = w_in.astype(BF16), w_att_up.astype(BF16), w_out.astype(BF16)
    p = dict(conv_w=conv_w, conv_b=rows(conv_b), conv_ln_g=rows(conv_ln_g), conv_ln_b=rows(conv_ln_b),
             conv_pw=conv_pw.astype(BF16), w_conv_up=w_conv_up.astype(BF16), pool_w=pool_w.astype(BF16),
             pool_b=rows(pool_b), pool_scale=rows(pool_scale), w_pool_up=w_pool_up.astype(BF16))

    xp, xs = x_prompt, x_sample
    k_all = v_all = None
    outs = [[] for _ in range(6)]
    for l in range(depth):
        lam_init = 0.8 - 0.6 * math.exp(-0.3 * l)
        final = l == depth - 1
        lam_p = jnp.stack([lambda_q1[l], lambda_k1[l], lambda_q2[l], lambda_k2[l]])

        q, k_all, kb, v_all, vb, ag, sg = _qkv_proj(xp, ng, w_in_b, cos_p, sin_p, l, tm=512,
                                                    kv_stack=(k_all, v_all))
        mcp, c_n, pl_n = _conv_pool(xp, ng, w_in_b, conv0, pool0, p, l, tm=512, pos_base=0)
        o = _prompt_attention(lam_p, q, kb, vb, lam_init, tq=512)
        flat = lambda a: a.reshape(bp * s, D_MODEL)
        xp = _out_proj(flat(o), flat(ag), flat(sg), flat(mcp), flat(xp), sub_g, watt, wo, fg,
                       l, lam_init, final, tm=512).reshape(bp, s, D_MODEL)
        outs[0].append(c_n)
        outs[1].append(pl_n)

        xs_rows = xs.reshape(1, bd * t, D_MODEL)
        q, k, kb, v, vb, ag, sg = _qkv_proj(xs_rows, ng, w_in_b, cos_s, sin_s, l, tm=bd * t)
        mcp, c_n, pl_n = _conv_pool(xs, ng, w_in_b, state_conv[l], state_pool[l], p, l, tm=t, pos_base=past)
        per_seq = lambda a: a.reshape(bd, t * N_HEADS, V_HEAD_DIM)
        o = _sample_attention(page_table, lam_p, per_seq(q), per_seq(k), per_seq(v), cache_k, cache_v, l, lam_init)
        flat = lambda a: a.reshape(bd * t, D_MODEL)
        xs = _out_proj(flat(o), flat(ag), flat(sg), flat(mcp), flat(xs), sub_g, watt, wo, fg,
                       l, lam_init, final, tm=bd * t).reshape(bd, t, D_MODEL)
        outs[2].append(k.reshape(bd, t, N_HEADS, V_HEAD_DIM))
        outs[3].append(v.reshape(bd, t, N_HEADS, V_HEAD_DIM))
        outs[4].append(c_n)
        outs[5].append(pl_n)

    st = [jnp.stack(o) for o in outs]
    return (xp, xs, k_all, v_all, st[0], st[1], st[2], st[3], st[4], st[5])
```

```python
import functools
import math

import jax
import jax.numpy as jnp
from jax import lax
from jax.experimental import pallas as pl
from jax.experimental.pallas import tpu as pltpu

F32 = jnp.float32
BF16 = jnp.bfloat16

D_MODEL = 1024
W_CONV = 512
W_ATT = 1024
W_POOL = 512
N_HEADS = 8
HEAD_DIM = 64
V_HEAD_DIM = 128
ROPE_THETA = 10000.0
CONV_WIDTH = 31
CONV_BUF = CONV_WIDTH - 1
POOL_WINDOWS = (2, 4, 8, 16)
POOL_GROUP = 128
POOL_MAX = 16
POOL_BUF = POOL_MAX - 1
EPS = 1e-6
LANES = 128
SUBLANES = 8
CONV_HALO = 32
POOL_HALO = 16
VMEM_LIMIT = 56 * 1024 * 1024
LOG2_E = math.log2(math.e)
CONV_ROWS = 32
PAGES_PER_STEP = 16
ATTN_STRIP = 64
ATTN_UNROLL = 4

_C_VAL, _C_GLU, _C_GATE = 0, 512, 1024
_Q, _K, _V, _A_GATE = 1536, 2560, 3584, 4608
_PL_IN, _PL_GATE = 5632, 6144
_G_CONV, _G_ATT, _G_POOL = 6656, 7680, 8704


def _dot(a, b):
    return jnp.dot(a, b, preferred_element_type=F32)


def _dot_nt(a, b):
    return lax.dot_general(a, b, (((1,), (1,)), ((), ())), preferred_element_type=F32)


def _sigmoid(x):
    return 1.0 / (1.0 + jnp.exp(-x))


def _silu(x):
    return x * _sigmoid(x)


def _rms_rows(x, g):
    return x * lax.rsqrt(jnp.mean(x * x, axis=-1, keepdims=True) + EPS) * g


def _const_spec(shape):
    nd = len(shape)
    return pl.BlockSpec(shape, lambda *_: (0,) * nd, pipeline_mode=pl.Buffered(1))


def _layer_spec(arr, layer):
    rest = arr.shape[1:]
    return pl.BlockSpec((None,) + rest, lambda *_: (layer,) + (0,) * len(rest), pipeline_mode=pl.Buffered(1))


def _params(*sem):
    return pltpu.CompilerParams(dimension_semantics=sem, vmem_limit_bytes=VMEM_LIMIT)


def _qkv_kernel(x_ref, g_ref, w_ref, cos_ref, sin_ref, *refs, per_head):
    q_ref, k_ref, kb_ref, v_ref, vb_ref, ag_ref, sg_ref = refs[-7:]
    h = _rms_rows(x_ref[...], g_ref[...]).astype(BF16)

    def proj(off):
        return _dot(h, w_ref[:, off:off + W_ATT])

    def store_f32(ref, c, val):
        if per_head:
            ref[:, c, :] = val
        else:
            ref[:, c * LANES:(c + 1) * LANES] = val

    cos = cos_ref[...]
    sin = sin_ref[...]
    first_half = (lax.broadcasted_iota(jnp.int32, cos.shape, 1) % HEAD_DIM) < (HEAD_DIM // 2)

    def rope(z):
        partner = jnp.where(first_half, pltpu.roll(z, LANES - HEAD_DIM // 2, 1),
                            pltpu.roll(z, HEAD_DIM // 2, 1))
        return z * cos + partner * sin

    scale = HEAD_DIM ** -0.5 * LOG2_E
    zq = proj(_Q)
    for c in range(W_ATT // LANES):
        sl = slice(c * LANES, (c + 1) * LANES)
        q_ref[:, sl] = (rope(zq[:, sl]) * scale).astype(BF16)
    zk = proj(_K)
    for c in range(W_ATT // LANES):
        sl = slice(c * LANES, (c + 1) * LANES)
        kr = rope(zk[:, sl])
        store_f32(k_ref, c, kr)
        kb_ref[:, sl] = kr.astype(BF16)
    zv = proj(_V)
    for c in range(W_ATT // LANES):
        store_f32(v_ref, c, zv[:, c * LANES:(c + 1) * LANES])
    vb_ref[...] = zv.astype(BF16)
    ag_ref[...] = _silu(proj(_A_GATE)).astype(BF16)
    sg_ref[...] = _sigmoid(proj(_G_ATT)).astype(BF16)


def _qkv_proj(x, norm_g, w_in, cos_tab, sin_tab, layer, tm, kv_stack=None):
    nb, n, _ = x.shape
    depth = w_in.shape[0]
    row = lambda b, i: (b, i, 0)
    tab = lambda b, i: (i, 0)
    blk = pl.BlockSpec((None, tm, D_MODEL), row)
    shp = lambda dt: jax.ShapeDtypeStruct((nb, n, W_ATT), dt)
    kv_spec, kv_shape, alias_in, aliases = blk, shp(F32), [], {}
    if kv_stack is not None:
        k_all, v_all = kv_stack
        kv_spec = pl.BlockSpec((None, None, tm, N_HEADS, V_HEAD_DIM), lambda b, i: (layer, b, i, 0, 0))
        kv_shape = jax.ShapeDtypeStruct((depth, nb, n, N_HEADS, V_HEAD_DIM), F32)
        if k_all is not None:
            alias_in = [k_all, v_all]
            aliases = {5: 1, 6: 3}
    return pl.pallas_call(
        functools.partial(_qkv_kernel, per_head=kv_stack is not None),
        grid=(nb, n // tm),
        in_specs=[blk, _layer_spec(norm_g, layer), _layer_spec(w_in, layer),
                  pl.BlockSpec((tm, LANES), tab), pl.BlockSpec((tm, LANES), tab)]
                 + [pl.BlockSpec(memory_space=pl.ANY)] * len(alias_in),
        out_specs=[blk, kv_spec, blk, kv_spec, blk, blk, blk],
        out_shape=[shp(BF16), kv_shape, shp(BF16), kv_shape, shp(BF16), shp(BF16), shp(BF16)],
        input_output_aliases=aliases,
        compiler_params=_params("parallel", "parallel"),
        name="qkv_proj",
    )(x, norm_g, w_in, cos_tab, sin_tab, *alias_in)


def _conv_pool_kernel(x_ref, g_ref, w_ref, cinit_ref, pinit_ref,
                      cw_ref, cb_ref, lng_ref, lnb_ref, pw_ref, cup_ref,
                      poolw_ref, poolb_ref, pscale_ref, pup_ref,
                      mcp_ref, cstate_ref, pstate_ref, uext, pext, ushift, *, tm, nt, pos_base):
    t = pl.program_id(1)

    @pl.when(t == 0)
    def _():
        uext[0:CONV_HALO - CONV_BUF, :] = jnp.zeros((CONV_HALO - CONV_BUF, W_CONV), F32)
        uext[CONV_HALO - CONV_BUF:CONV_HALO, :] = cinit_ref[...]
        pext[0:POOL_HALO - POOL_BUF, :] = jnp.zeros((POOL_HALO - POOL_BUF, W_POOL), F32)
        pext[POOL_HALO - POOL_BUF:POOL_HALO, :] = pinit_ref[...]

    h = _rms_rows(x_ref[...], g_ref[...]).astype(BF16)

    def proj(off, width):
        return _dot(h, w_ref[:, off:off + width])

    u = proj(_C_VAL, W_CONV) * _sigmoid(proj(_C_GLU, W_CONV))
    uext[CONV_HALO:CONV_HALO + tm, :] = u
    first = CONV_HALO - CONV_BUF
    n_shift = tm + CONV_HALO - SUBLANES
    for b in range(1, SUBLANES):
        ushift[b - 1, 0:n_shift, :] = uext[b:b + n_shift, :]
    rc = min(CONV_ROWS, tm)
    chunks = []
    for r in range(0, tm, rc):
        acc = jnp.zeros((rc, W_CONV), F32) + cb_ref[...]
        for j in range(CONV_WIDTH):
            b = (first + j) % SUBLANES
            at = first + j - b + r
            win = uext[at:at + rc, :] if b == 0 else ushift[b - 1, at:at + rc, :]
            acc = acc + win * cw_ref[j:j + 1, :]
        xc = acc - jnp.mean(acc, axis=-1, keepdims=True)
        c = xc * lax.rsqrt(jnp.mean(xc * xc, axis=-1, keepdims=True) + EPS) * lng_ref[...] + lnb_ref[...]
        chunks.append(_silu(c).astype(BF16))
    c = jnp.concatenate(chunks, axis=0)
    c = _dot(c, pw_ref[...]) * _silu(proj(_C_GATE, W_CONV))
    br_conv = _dot(c.astype(BF16), cup_ref[...])

    pl_in = proj(_PL_IN, W_POOL)
    pext[POOL_HALO:POOL_HALO + tm, :] = pl_in
    pos = pos_base + t * tm + lax.broadcasted_iota(jnp.int32, (tm, POOL_GROUP), 0)
    pooled = []
    for g, w in enumerate(POOL_WINDOWS):
        cols = slice(g * POOL_GROUP, (g + 1) * POOL_GROUP)
        tok = pext[POOL_HALO:POOL_HALO + tm, cols]
        win = tok
        for i in range(1, w):
            win = win + pext[POOL_HALO - i:POOL_HALO - i + tm, cols]
        cnt = jnp.minimum(w, pos + 1).astype(F32)
        m = (win / cnt - tok).astype(BF16)
        pooled.append(_dot(m, poolw_ref[g]))
    m = jnp.concatenate(pooled, axis=-1) + poolb_ref[...]
    m = m * pscale_ref[...] * _silu(proj(_PL_GATE, W_POOL))
    br_pool = _dot(m.astype(BF16), pup_ref[...])

    mcp_ref[...] = (_sigmoid(proj(_G_CONV, D_MODEL)) * br_conv
                    + _sigmoid(proj(_G_POOL, D_MODEL)) * br_pool)

    @pl.when(t == nt - 1)
    def _():
        cstate_ref[...] = uext[tm + CONV_HALO - CONV_BUF:tm + CONV_HALO, :]
        pstate_ref[...] = pext[tm + POOL_HALO - POOL_BUF:tm + POOL_HALO, :]

    if nt > 1:
        uext[0:CONV_HALO, :] = uext[tm:tm + CONV_HALO, :]
        pext[0:POOL_HALO, :] = pext[tm:tm + POOL_HALO, :]


def _conv_pool(x, norm_g, w_in, conv_init, pool_init, p, layer, tm, pos_base):
    nb, n, _ = x.shape
    nt = n // tm
    row = lambda b, i: (b, i, 0)
    per_b = lambda b, i: (b, 0, 0)
    consts = [p['conv_w'], p['conv_b'], p['conv_ln_g'], p['conv_ln_b'], p['conv_pw'], p['w_conv_up'],
              p['pool_w'], p['pool_b'], p['pool_scale'], p['w_pool_up']]
    return pl.pallas_call(
        functools.partial(_conv_pool_kernel, tm=tm, nt=nt, pos_base=pos_base),
        grid=(nb, nt),
        in_specs=[pl.BlockSpec((None, tm, D_MODEL), row), _layer_spec(norm_g, layer), _layer_spec(w_in, layer),
                  pl.BlockSpec((None, CONV_BUF, W_CONV), per_b), pl.BlockSpec((None, POOL_BUF, W_POOL), per_b)]
                 + [_layer_spec(c, layer) for c in consts],
        out_specs=[pl.BlockSpec((None, tm, D_MODEL), row),
                   pl.BlockSpec((None, CONV_BUF, W_CONV), per_b), pl.BlockSpec((None, POOL_BUF, W_POOL), per_b)],
        out_shape=[jax.ShapeDtypeStruct((nb, n, D_MODEL), F32),
                   jax.ShapeDtypeStruct((nb, CONV_BUF, W_CONV), F32),
                   jax.ShapeDtypeStruct((nb, POOL_BUF, W_POOL), F32)],
        scratch_shapes=[pltpu.VMEM((CONV_HALO + tm, W_CONV), F32), pltpu.VMEM((POOL_HALO + tm, W_POOL), F32),
                        pltpu.VMEM((SUBLANES - 1, CONV_HALO + tm - SUBLANES, W_CONV), F32)],
        compiler_params=_params("parallel", "arbitrary"),
        name="conv_pool",
    )(x, norm_g, w_in, conv_init, pool_init, *consts)


def _lambda(lam_ref, lam_init):
    lp = lam_ref[...]
    return (jnp.exp(jnp.sum(lp[0:1] * lp[1:2], axis=-1, keepdims=True))
            - jnp.exp(jnp.sum(lp[2:3] * lp[3:4], axis=-1, keepdims=True)) + lam_init)


def _split_branches(q):
    lane = lax.broadcasted_iota(jnp.int32, q.shape, 1)
    zero = jnp.zeros_like(q)
    return jnp.concatenate([jnp.where(lane < HEAD_DIM, q, zero), jnp.where(lane >= HEAD_DIM, q, zero)], axis=0)


def _tree(op, xs):
    while len(xs) > 1:
        xs = [op(xs[i], xs[i + 1]) if i + 1 < len(xs) else xs[i] for i in range(0, len(xs), 2)]
    return xs[0]


def _online_softmax_step(scores, values, m_scr, l_scr, acc_scr):
    chunks = [s[:, c:c + LANES] for s in scores for c in range(0, s.shape[1], LANES)]
    m_prev = m_scr[...]
    m_new = jnp.maximum(m_prev, jnp.max(_tree(jnp.maximum, chunks), axis=-1, keepdims=True))
    alpha = jnp.exp2(m_prev - m_new)
    probs = [jnp.exp2(ch - m_new) for ch in chunks]
    l_scr[...] = alpha * l_scr[...] + _tree(jnp.add, probs)
    pv, at = None, 0
    for s, v in zip(scores, values):
        n = s.shape[1] // LANES
        p = jnp.concatenate([x.astype(BF16) for x in probs[at:at + n]], axis=1)
        at += n
        pv = _dot(p, v) if pv is None else pv + _dot(p, v)
    acc_scr[...] = alpha * acc_scr[...] + pv
    m_scr[...] = m_new


def _init_softmax(m_scr, l_scr, acc_scr):
    m_scr[...] = jnp.full(m_scr.shape, -jnp.inf, F32)
    l_scr[...] = jnp.zeros(l_scr.shape, F32)
    acc_scr[...] = jnp.zeros(acc_scr.shape, F32)


def _finish_softmax(lam_ref, lam_init, l_scr, acc_scr):
    o = acc_scr[...] / jnp.sum(l_scr[...], axis=-1, keepdims=True)
    half = o.shape[0] // 2
    return o[:half] - _lambda(lam_ref, lam_init) * o[half:]


def _prompt_attn_kernel(lam_ref, q_ref, k_ref, v_ref, o_ref, sa_scr, sb_scr, pa_scr, pb_scr, m_scr, l_scr,
                        acc_scr, *, tq, lam_init):
    qi = pl.program_id(2)
    q2 = _split_branches(q_ref[...])
    _init_softmax(m_scr, l_scr, acc_scr)

    def rows_of(j):
        return pl.ds(pl.multiple_of(j * tq, tq), tq)

    def scores(j):
        return _dot_nt(q2, k_ref[rows_of(j), :])

    def update(s_scr, p_scr, j, diagonal):
        for r0 in range(0, 2 * tq, ATTN_STRIP):
            rows = slice(r0, r0 + ATTN_STRIP)
            s = s_scr[rows, :]
            if diagonal:
                qpos = r0 % tq + lax.broadcasted_iota(jnp.int32, s.shape, 0)
                kpos = lax.broadcasted_iota(jnp.int32, s.shape, 1)
                s = jnp.where(kpos <= qpos, s, -jnp.inf)
            chunks = [s[:, c:c + LANES] for c in range(0, tq, LANES)]
            m_prev = m_scr[rows, :]
            m_new = jnp.maximum(m_prev, jnp.max(_tree(jnp.maximum, chunks), axis=-1, keepdims=True))
            alpha = jnp.exp2(m_prev - m_new)
            probs = [jnp.exp2(ch - m_new) for ch in chunks]
            l_scr[rows, :] = alpha * l_scr[rows, :] + _tree(jnp.add, probs)
            acc_scr[rows, :] = alpha * acc_scr[rows, :]
            m_scr[rows, :] = m_new
            p_scr[rows, :] = jnp.concatenate([x.astype(BF16) for x in probs], axis=1)
        acc_scr[...] += _dot(p_scr[...], v_ref[rows_of(j), :])

    bufs = ((sa_scr, pa_scr), (sb_scr, pb_scr))

    def run(first, count, last_is_diagonal, prefetch_after):
        for i in range(count):
            if i + 1 < count or prefetch_after:
                bufs[(i + 1) % 2][0][...] = scores(first + i + 1)
            update(*bufs[i % 2], first + i, last_is_diagonal and i + 1 == count)

    sa_scr[...] = scores(0)

    def full_blocks(jj, carry):
        run(ATTN_UNROLL * jj, ATTN_UNROLL, False, True)
        return carry

    lax.fori_loop(0, qi // ATTN_UNROLL, full_blocks, 0)

    for rem in range(ATTN_UNROLL):
        @pl.when(qi % ATTN_UNROLL == rem)
        def _(rem=rem):
            run(qi - rem, rem + 1, True, False)

    o_ref[...] = _finish_softmax(lam_ref, lam_init, l_scr, acc_scr)


def _prompt_attention(lam_p, q, k, v, lam_init, tq):
    nb, s, _ = q.shape
    return pl.pallas_call(
        functools.partial(_prompt_attn_kernel, tq=tq, lam_init=lam_init),
        grid=(nb, N_HEADS, s // tq),
        in_specs=[_const_spec(lam_p.shape),
                  pl.BlockSpec((None, tq, V_HEAD_DIM), lambda b, h, i: (b, i, h)),
                  pl.BlockSpec((None, s, V_HEAD_DIM), lambda b, h, i: (b, 0, h)),
                  pl.BlockSpec((None, s, V_HEAD_DIM), lambda b, h, i: (b, 0, h))],
        out_specs=pl.BlockSpec((None, tq, V_HEAD_DIM), lambda b, h, i: (b, i, h)),
        out_shape=jax.ShapeDtypeStruct((nb, s, W_ATT), F32),
        scratch_shapes=[pltpu.VMEM((2 * tq, tq), F32), pltpu.VMEM((2 * tq, tq), F32),
                        pltpu.VMEM((2 * tq, tq), BF16), pltpu.VMEM((2 * tq, tq), BF16),
                        pltpu.VMEM((2 * tq, LANES), F32), pltpu.VMEM((2 * tq, LANES), F32),
                        pltpu.VMEM((2 * tq, V_HEAD_DIM), F32)],
        compiler_params=_params("parallel", "parallel", "arbitrary"),
        name="prompt_attention",
    )(lam_p, q, k, v)


def _sample_attn_kernel(pt_ref, lam_ref, bias_ref, bias_new_ref, q_ref, kn_ref, vn_ref, *refs,
                        n_steps, n_group, lam_init):
    kc_refs, vc_refs = refs[:n_group], refs[n_group:2 * n_group]
    o_ref, m_scr, l_scr, acc_scr = refs[2 * n_group:]
    j = pl.program_id(1)
    q2 = _split_branches(q_ref[...])

    @pl.when(j == 0)
    def _():
        _init_softmax(m_scr, l_scr, acc_scr)

    def flat(ref):
        page, heads, d = ref.shape
        return ref[...].reshape(page * heads, d).astype(BF16)

    bias = bias_ref[...]
    scores = [_dot_nt(q2, flat(kc)) + bias for kc in kc_refs]
    _online_softmax_step(scores, [flat(vc) for vc in vc_refs], m_scr, l_scr, acc_scr)

    @pl.when(j == n_steps - 1)
    def _():
        s_new = _dot_nt(q2, kn_ref[...].astype(BF16)) + bias_new_ref[...]
        _online_softmax_step([s_new], [vn_ref[...].astype(BF16)], m_scr, l_scr, acc_scr)
        o_ref[...] = _finish_softmax(lam_ref, lam_init, l_scr, acc_scr)


def _head_bias(n_rows, n_cols, t_new, n_valid):
    r = jnp.arange(n_rows)[:, None]
    c = jnp.arange(n_cols)[None, :]
    ok = (r % N_HEADS == c % N_HEADS) & (c < n_valid)
    if t_new is not None:
        ok &= c // N_HEADS <= (r // N_HEADS) % t_new
    return jnp.where(ok, 0.0, -jnp.inf).astype(F32)


def _sample_attention(page_table, lam_p, q, k_new, v_new, cache_k, cache_v, layer, lam_init):
    nb, n_pages = page_table.shape
    page = cache_k.shape[2]
    n_new = q.shape[1]
    t_new = n_new // N_HEADS
    n_group = PAGES_PER_STEP
    n_steps = n_pages // n_group
    rows = 2 * n_new
    pad = ((0, 0), (0, LANES - n_new), (0, 0))
    k_new, v_new = jnp.pad(k_new, pad), jnp.pad(v_new, pad)
    bias = _head_bias(rows, page * N_HEADS, None, page * N_HEADS)
    bias_new = _head_bias(rows, LANES, t_new, n_new)
    per_b = lambda b, j, pt: (b, 0, 0)
    const = lambda b, j, pt: (0, 0)

    def page_spec(g):
        return pl.BlockSpec((None, None, page, N_HEADS, V_HEAD_DIM),
                            lambda b, j, pt: (layer, pt[b, j * n_group + g], 0, 0, 0))

    new_spec = pl.BlockSpec((None, LANES, V_HEAD_DIM), per_b)
    pages = [page_spec(g) for g in range(n_group)]
    return pl.pallas_call(
        functools.partial(_sample_attn_kernel, n_steps=n_steps, n_group=n_group, lam_init=lam_init),
        grid_spec=pltpu.PrefetchScalarGridSpec(
            num_scalar_prefetch=1,
            grid=(nb, n_steps),
            in_specs=[pl.BlockSpec(lam_p.shape, const), pl.BlockSpec(bias.shape, const),
                      pl.BlockSpec(bias_new.shape, const),
                      pl.BlockSpec((None, n_new, V_HEAD_DIM), per_b), new_spec, new_spec] + pages + pages,
            out_specs=pl.BlockSpec((None, n_new, V_HEAD_DIM), per_b),
            scratch_shapes=[pltpu.VMEM((rows, LANES), F32), pltpu.VMEM((rows, LANES), F32),
                            pltpu.VMEM((rows, V_HEAD_DIM), F32)]),
        out_shape=jax.ShapeDtypeStruct((nb, n_new, V_HEAD_DIM), F32),
        compiler_params=_params("parallel", "arbitrary"),
        name="sample_attention",
    )(page_table, lam_p, bias, bias_new, q, k_new, v_new, *([cache_k] * n_group), *([cache_v] * n_group))


def _out_kernel(o_ref, ag_ref, sg_ref, mcp_ref, x_ref, subg_ref, watt_ref, wout_ref, fg_ref, y_ref,
                *, lam_init, final):
    heads = []
    for hd in range(N_HEADS):
        oh = o_ref[:, hd * V_HEAD_DIM:(hd + 1) * V_HEAD_DIM]
        heads.append(_rms_rows(oh, subg_ref[...]) * (1.0 - lam_init))
    o = jnp.concatenate(heads, axis=-1) * ag_ref[...].astype(F32)
    br_att = _dot(o.astype(BF16), watt_ref[...])
    merged = sg_ref[...].astype(F32) * br_att + mcp_ref[...]
    y = x_ref[...] + _dot(merged.astype(BF16), wout_ref[...])
    y_ref[...] = _rms_rows(y, fg_ref[...]) if final else y


def _out_proj(o, ag, sg, mcp, x, subln_g, w_att_up, w_out, final_g, layer, lam_init, final, tm):
    n = x.shape[0]
    row = lambda i: (i, 0)
    blk = pl.BlockSpec((tm, D_MODEL), row)
    return pl.pallas_call(
        functools.partial(_out_kernel, lam_init=lam_init, final=final),
        grid=(n // tm,),
        in_specs=[blk] * 5 + [_layer_spec(subln_g, layer), _layer_spec(w_att_up, layer),
                              _layer_spec(w_out, layer), _const_spec(final_g.shape)],
        out_specs=blk,
        out_shape=jax.ShapeDtypeStruct((n, D_MODEL), F32),
        compiler_params=_params("parallel"),
        name="out_proj",
    )(o, ag, sg, mcp, x, subln_g, w_att_up, w_out, final_g)


def _rope_tables(pos):
    half = HEAD_DIM // 2
    inv = ROPE_THETA ** (-jnp.arange(half, dtype=F32) / half)
    ang = pos.astype(F32)[:, None] * inv[None, :]
    cos, sin = jnp.cos(ang), jnp.sin(ang)
    reps = LANES // HEAD_DIM
    return (jnp.tile(jnp.concatenate([cos, cos], axis=-1), (1, reps)),
            jnp.tile(jnp.concatenate([-sin, sin], axis=-1), (1, reps)))


def kernel(x_prompt, x_sample, cache_k, cache_v, state_conv, state_pool, page_table, norm_g, w_in, conv_w, conv_b,
           conv_ln_g, conv_ln_b, conv_pw, lambda_q1, lambda_k1, lambda_q2, lambda_k2, subln_g, pool_w, pool_b,
           pool_scale, w_conv_up, w_att_up, w_pool_up, w_out, final_g):
    bp, s, _ = x_prompt.shape
    bd, t, _ = x_sample.shape
    depth = w_in.shape[0]
    past = page_table.shape[1] * cache_k.shape[2]
    cos_p, sin_p = _rope_tables(jnp.arange(s, dtype=jnp.int32))
    cos_s, sin_s = _rope_tables(jnp.tile(past + jnp.arange(t, dtype=jnp.int32), bd))
    conv0 = jnp.zeros((bp, CONV_BUF, W_CONV), F32)
    pool0 = jnp.zeros((bp, POOL_BUF, W_POOL), F32)
    fg = final_g.reshape(1, D_MODEL)

    rows = lambda a: a.reshape(depth, 1, -1)
    ng, sub_g = rows(norm_g), rows(subln_g)
    w_in_b, watt, wo = w_in.astype(BF16), w_att_up.astype(BF16), w_out.astype(BF16)
    p = dict(conv_w=conv_w, conv_b=rows(conv_b), conv_ln_g=rows(conv_ln_g), conv_ln_b=rows(conv_ln_b),
             conv_pw=conv_pw.astype(BF16), w_conv_up=w_conv_up.astype(BF16), pool_w=pool_w.astype(BF16),
             pool_b=rows(pool_b), pool_scale=rows(pool_scale), w_pool_up=w_pool_up.astype(BF16))

    xp, xs = x_prompt, x_sample
    k_all = v_all = None
    outs = [[] for _ in range(6)]
    for l in range(depth):
        lam_init = 0.8 - 0.6 * math.exp(-0.3 * l)
        final = l == depth - 1
        lam_p = jnp.stack([lambda_q1[l], lambda_k1[l], lambda_q2[l], lambda_k2[l]])

        q, k_all, kb, v_all, vb, ag, sg = _qkv_proj(xp, ng, w_in_b, cos_p, sin_p, l, tm=512,
                                                    kv_stack=(k_all, v_all))
        mcp, c_n, pl_n = _conv_pool(xp, ng, w_in_b, conv0, pool0, p, l, tm=512, pos_base=0)
        o = _prompt_attention(lam_p, q, kb, vb, lam_init, tq=512)
        flat = lambda a: a.reshape(bp * s, D_MODEL)
        xp = _out_proj(flat(o), flat(ag), flat(sg), flat(mcp), flat(xp), sub_g, watt, wo, fg,
                       l, lam_init, final, tm=512).reshape(bp, s, D_MODEL)
        outs[0].append(c_n)
        outs[1].append(pl_n)

        xs_rows = xs.reshape(1, bd * t, D_MODEL)
        q, k, kb, v, vb, ag, sg = _qkv_proj(xs_rows, ng, w_in_b, cos_s, sin_s, l, tm=bd * t)
        mcp, c_n, pl_n = _conv_pool(xs, ng, w_in_b, state_conv[l], state_pool[l], p, l, tm=t, pos_base=past)
        per_seq = lambda a: a.reshape(bd, t * N_HEADS, V_HEAD_DIM)
        o = _sample_attention(page_table, lam_p, per_seq(q), per_seq(k), per_seq(v), cache_k, cache_v, l, lam_init)
        flat = lambda a: a.reshape(bd * t, D_MODEL)
        xs = _out_proj(flat(o), flat(ag), flat(sg), flat(mcp), flat(xs), sub_g, watt, wo, fg,
                       l, lam_init, final, tm=bd * t).reshape(bd, t, D_MODEL)
        outs[2].append(k.reshape(bd, t, N_HEADS, V_HEAD_DIM))
        outs[3].append(v.reshape(bd, t, N_HEADS, V_HEAD_DIM))
        outs[4].append(c_n)
        outs[5].append(pl_n)

    st = [jnp.stack(o) for o in outs]
    return (xp, xs, k_all, v_all, st[0], st[1], st[2], st[3], st[4], st[5])
```

```python
import functools
import math

import jax
import jax.numpy as jnp
from jax import lax
from jax.experimental import pallas as pl
from jax.experimental.pallas import tpu as pltpu

F32 = jnp.float32
BF16 = jnp.bfloat16

D_MODEL = 1024
W_CONV = 512
W_ATT = 1024
W_POOL = 512
N_HEADS = 8
HEAD_DIM = 64
V_HEAD_DIM = 128
ROPE_THETA = 10000.0
CONV_WIDTH = 31
CONV_BUF = CONV_WIDTH - 1
POOL_WINDOWS = (2, 4, 8, 16)
POOL_GROUP = 128
POOL_MAX = 16
POOL_BUF = POOL_MAX - 1
EPS = 1e-6
LANES = 128
SUBLANES = 8
CONV_HALO = 32
POOL_HALO = 16
VMEM_LIMIT = 56 * 1024 * 1024
LOG2_E = math.log2(math.e)
CONV_ROWS = 32
PAGES_PER_STEP = 16
ATTN_STRIP = 64
ATTN_UNROLL = 8

_C_VAL, _C_GLU, _C_GATE = 0, 512, 1024
_Q, _K, _V, _A_GATE = 1536, 2560, 3584, 4608
_PL_IN, _PL_GATE = 5632, 6144
_G_CONV, _G_ATT, _G_POOL = 6656, 7680, 8704


def _dot(a, b):
    return jnp.dot(a, b, preferred_element_type=F32)


def _dot_nt(a, b):
    return lax.dot_general(a, b, (((1,), (1,)), ((), ())), preferred_element_type=F32)


def _sigmoid(x):
    return 1.0 / (1.0 + jnp.exp(-x))


def _silu(x):
    return x * _sigmoid(x)


def _rms_rows(x, g):
    return x * lax.rsqrt(jnp.mean(x * x, axis=-1, keepdims=True) + EPS) * g


def _const_spec(shape):
    nd = len(shape)
    return pl.BlockSpec(shape, lambda *_: (0,) * nd, pipeline_mode=pl.Buffered(1))


def _layer_spec(arr, layer):
    rest = arr.shape[1:]
    return pl.BlockSpec((None,) + rest, lambda *_: (layer,) + (0,) * len(rest), pipeline_mode=pl.Buffered(1))


def _params(*sem):
    return pltpu.CompilerParams(dimension_semantics=sem, vmem_limit_bytes=VMEM_LIMIT)


def _qkv_kernel(x_ref, g_ref, w_ref, cos_ref, sin_ref, *refs, per_head):
    q_ref, k_ref, kb_ref, v_ref, vb_ref, ag_ref, sg_ref = refs[-7:]
    h = _rms_rows(x_ref[...], g_ref[...]).astype(BF16)

    def proj(off):
        return _dot(h, w_ref[:, off:off + W_ATT])

    def store_f32(ref, c, val):
        if per_head:
            ref[:, c, :] = val
        else:
            ref[:, c * LANES:(c + 1) * LANES] = val

    cos = cos_ref[...]
    sin = sin_ref[...]
    first_half = (lax.broadcasted_iota(jnp.int32, cos.shape, 1) % HEAD_DIM) < (HEAD_DIM // 2)

    def rope(z):
        partner = jnp.where(first_half, pltpu.roll(z, LANES - HEAD_DIM // 2, 1),
                            pltpu.roll(z, HEAD_DIM // 2, 1))
        return z * cos + partner * sin

    scale = HEAD_DIM ** -0.5 * LOG2_E
    zq = proj(_Q)
    for c in range(W_ATT // LANES):
        sl = slice(c * LANES, (c + 1) * LANES)
        q_ref[:, sl] = (rope(zq[:, sl]) * scale).astype(BF16)
    zk = proj(_K)
    for c in range(W_ATT // LANES):
        sl = slice(c * LANES, (c + 1) * LANES)
        kr = rope(zk[:, sl])
        store_f32(k_ref, c, kr)
        kb_ref[:, sl] = kr.astype(BF16)
    zv = proj(_V)
    for c in range(W_ATT // LANES):
        store_f32(v_ref, c, zv[:, c * LANES:(c + 1) * LANES])
    vb_ref[...] = zv.astype(BF16)
    ag_ref[...] = _silu(proj(_A_GATE)).astype(BF16)
    sg_ref[...] = _sigmoid(proj(_G_ATT)).astype(BF16)


def _qkv_proj(x, norm_g, w_in, cos_tab, sin_tab, layer, tm, kv_stack=None):
    nb, n, _ = x.shape
    depth = w_in.shape[0]
    row = lambda b, i: (b, i, 0)
    tab = lambda b, i: (i, 0)
    blk = pl.BlockSpec((None, tm, D_MODEL), row)
    shp = lambda dt: jax.ShapeDtypeStruct((nb, n, W_ATT), dt)
    kv_spec, kv_shape, alias_in, aliases = blk, shp(F32), [], {}
    if kv_stack is not None:
        k_all, v_all = kv_stack
        kv_spec = pl.BlockSpec((None, None, tm, N_HEADS, V_HEAD_DIM), lambda b, i: (layer, b, i, 0, 0))
        kv_shape = jax.ShapeDtypeStruct((depth, nb, n, N_HEADS, V_HEAD_DIM), F32)
        if k_all is not None:
            alias_in = [k_all, v_all]
            aliases = {5: 1, 6: 3}
    return pl.pallas_call(
        functools.partial(_qkv_kernel, per_head=kv_stack is not None),
        grid=(nb, n // tm),
        in_specs=[blk, _layer_spec(norm_g, layer), _layer_spec(w_in, layer),
                  pl.BlockSpec((tm, LANES), tab), pl.BlockSpec((tm, LANES), tab)]
                 + [pl.BlockSpec(memory_space=pl.ANY)] * len(alias_in),
        out_specs=[blk, kv_spec, blk, kv_spec, blk, blk, blk],
        out_shape=[shp(BF16), kv_shape, shp(BF16), kv_shape, shp(BF16), shp(BF16), shp(BF16)],
        input_output_aliases=aliases,
        compiler_params=_params("parallel", "parallel"),
        name="qkv_proj",
    )(x, norm_g, w_in, cos_tab, sin_tab, *alias_in)


def _conv_pool_kernel(x_ref, g_ref, w_ref, cinit_ref, pinit_ref,
                      cw_ref, cb_ref, lng_ref, lnb_ref, pw_ref, cup_ref,
                      poolw_ref, poolb_ref, pscale_ref, pup_ref,
                      mcp_ref, cstate_ref, pstate_ref, uext, pext, ushift, *, tm, nt, pos_base):
    t = pl.program_id(1)

    @pl.when(t == 0)
    def _():
        uext[0:CONV_HALO - CONV_BUF, :] = jnp.zeros((CONV_HALO - CONV_BUF, W_CONV), F32)
        uext[CONV_HALO - CONV_BUF:CONV_HALO, :] = cinit_ref[...]
        pext[0:POOL_HALO - POOL_BUF, :] = jnp.zeros((POOL_HALO - POOL_BUF, W_POOL), F32)
        pext[POOL_HALO - POOL_BUF:POOL_HALO, :] = pinit_ref[...]

    h = _rms_rows(x_ref[...], g_ref[...]).astype(BF16)

    def proj(off, width):
        return _dot(h, w_ref[:, off:off + width])

    u = proj(_C_VAL, W_CONV) * _sigmoid(proj(_C_GLU, W_CONV))
    uext[CONV_HALO:CONV_HALO + tm, :] = u
    first = CONV_HALO - CONV_BUF
    n_shift = tm + CONV_HALO - SUBLANES
    for b in range(1, SUBLANES):
        ushift[b - 1, 0:n_shift, :] = uext[b:b + n_shift, :]
    rc = min(CONV_ROWS, tm)
    chunks = []
    for r in range(0, tm, rc):
        acc = jnp.zeros((rc, W_CONV), F32) + cb_ref[...]
        for j in range(CONV_WIDTH):
            b = (first + j) % SUBLANES
            at = first + j - b + r
            win = uext[at:at + rc, :] if b == 0 else ushift[b - 1, at:at + rc, :]
            acc = acc + win * cw_ref[j:j + 1, :]
        xc = acc - jnp.mean(acc, axis=-1, keepdims=True)
        c = xc * lax.rsqrt(jnp.mean(xc * xc, axis=-1, keepdims=True) + EPS) * lng_ref[...] + lnb_ref[...]
        chunks.append(_silu(c).astype(BF16))
    c = jnp.concatenate(chunks, axis=0)
    c = _dot(c, pw_ref[...]) * _silu(proj(_C_GATE, W_CONV))
    br_conv = _dot(c.astype(BF16), cup_ref[...])

    pl_in = proj(_PL_IN, W_POOL)
    pext[POOL_HALO:POOL_HALO + tm, :] = pl_in
    pos = pos_base + t * tm + lax.broadcasted_iota(jnp.int32, (tm, POOL_GROUP), 0)
    pooled = []
    for g, w in enumerate(POOL_WINDOWS):
        cols = slice(g * POOL_GROUP, (g + 1) * POOL_GROUP)
        tok = pext[POOL_HALO:POOL_HALO + tm, cols]
        win = tok
        for i in range(1, w):
            win = win + pext[POOL_HALO - i:POOL_HALO - i + tm, cols]
        cnt = jnp.minimum(w, pos + 1).astype(F32)
        m = (win / cnt - tok).astype(BF16)
        pooled.append(_dot(m, poolw_ref[g]))
    m = jnp.concatenate(pooled, axis=-1) + poolb_ref[...]
    m = m * pscale_ref[...] * _silu(proj(_PL_GATE, W_POOL))
    br_pool = _dot(m.astype(BF16), pup_ref[...])

    mcp_ref[...] = (_sigmoid(proj(_G_CONV, D_MODEL)) * br_conv
                    + _sigmoid(proj(_G_POOL, D_MODEL)) * br_pool)

    @pl.when(t == nt - 1)
    def _():
        cstate_ref[...] = uext[tm + CONV_HALO - CONV_BUF:tm + CONV_HALO, :]
        pstate_ref[...] = pext[tm + POOL_HALO - POOL_BUF:tm + POOL_HALO, :]

    if nt > 1:
        uext[0:CONV_HALO, :] = uext[tm:tm + CONV_HALO, :]
        pext[0:POOL_HALO, :] = pext[tm:tm + POOL_HALO, :]


def _conv_pool(x, norm_g, w_in, conv_init, pool_init, p, layer, tm, pos_base):
    nb, n, _ = x.shape
    nt = n // tm
    row = lambda b, i: (b, i, 0)
    per_b = lambda b, i: (b, 0, 0)
    consts = [p['conv_w'], p['conv_b'], p['conv_ln_g'], p['conv_ln_b'], p['conv_pw'], p['w_conv_up'],
              p['pool_w'], p['pool_b'], p['pool_scale'], p['w_pool_up']]
    return pl.pallas_call(
        functools.partial(_conv_pool_kernel, tm=tm, nt=nt, pos_base=pos_base),
        grid=(nb, nt),
        in_specs=[pl.BlockSpec((None, tm, D_MODEL), row), _layer_spec(norm_g, layer), _layer_spec(w_in, layer),
                  pl.BlockSpec((None, CONV_BUF, W_CONV), per_b), pl.BlockSpec((None, POOL_BUF, W_POOL), per_b)]
                 + [_layer_spec(c, layer) for c in consts],
        out_specs=[pl.BlockSpec((None, tm, D_MODEL), row),
                   pl.BlockSpec((None, CONV_BUF, W_CONV), per_b), pl.BlockSpec((None, POOL_BUF, W_POOL), per_b)],
        out_shape=[jax.ShapeDtypeStruct((nb, n, D_MODEL), F32),
                   jax.ShapeDtypeStruct((nb, CONV_BUF, W_CONV), F32),
                   jax.ShapeDtypeStruct((nb, POOL_BUF, W_POOL), F32)],
        scratch_shapes=[pltpu.VMEM((CONV_HALO + tm, W_CONV), F32), pltpu.VMEM((POOL_HALO + tm, W_POOL), F32),
                        pltpu.VMEM((SUBLANES - 1, CONV_HALO + tm - SUBLANES, W_CONV), F32)],
        compiler_params=_params("parallel", "arbitrary"),
        name="conv_pool",
    )(x, norm_g, w_in, conv_init, pool_init, *consts)


def _lambda(lam_ref, lam_init):
    lp = lam_ref[...]
    return (jnp.exp(jnp.sum(lp[0:1] * lp[1:2], axis=-1, keepdims=True))
            - jnp.exp(jnp.sum(lp[2:3] * lp[3:4], axis=-1, keepdims=True)) + lam_init)


def _split_branches(q):
    lane = lax.broadcasted_iota(jnp.int32, q.shape, 1)
    zero = jnp.zeros_like(q)
    return jnp.concatenate([jnp.where(lane < HEAD_DIM, q, zero), jnp.where(lane >= HEAD_DIM, q, zero)], axis=0)


def _tree(op, xs):
    while len(xs) > 1:
        xs = [op(xs[i], xs[i + 1]) if i + 1 < len(xs) else xs[i] for i in range(0, len(xs), 2)]
    return xs[0]


def _online_softmax_step(scores, values, m_scr, l_scr, acc_scr):
    chunks = [s[:, c:c + LANES] for s in scores for c in range(0, s.shape[1], LANES)]
    m_prev = m_scr[...]
    m_new = jnp.maximum(m_prev, jnp.max(_tree(jnp.maximum, chunks), axis=-1, keepdims=True))
    alpha = jnp.exp2(m_prev - m_new)
    probs = [jnp.exp2(ch - m_new) for ch in chunks]
    l_scr[...] = alpha * l_scr[...] + _tree(jnp.add, probs)
    pv, at = None, 0
    for s, v in zip(scores, values):
        n = s.shape[1] // LANES
        p = jnp.concatenate([x.astype(BF16) for x in probs[at:at + n]], axis=1)
        at += n
        pv = _dot(p, v) if pv is None else pv + _dot(p, v)
    acc_scr[...] = alpha * acc_scr[...] + pv
    m_scr[...] = m_new


def _init_softmax(m_scr, l_scr, acc_scr):
    m_scr[...] = jnp.full(m_scr.shape, -jnp.inf, F32)
    l_scr[...] = jnp.zeros(l_scr.shape, F32)
    acc_scr[...] = jnp.zeros(acc_scr.shape, F32)


def _finish_softmax(lam_ref, lam_init, l_scr, acc_scr):
    o = acc_scr[...] / jnp.sum(l_scr[...], axis=-1, keepdims=True)
    half = o.shape[0] // 2
    return o[:half] - _lambda(lam_ref, lam_init) * o[half:]


def _prompt_attn_kernel(lam_ref, q_ref, k_ref, v_ref, o_ref, sa_scr, sb_scr, pa_scr, pb_scr, m_scr, l_scr,
                        acc_scr, *, tq, lam_init):
    qi = pl.program_id(2)
    q2 = _split_branches(q_ref[...])
    _init_softmax(m_scr, l_scr, acc_scr)

    def rows_of(j):
        return pl.ds(pl.multiple_of(j * tq, tq), tq)

    def scores(j):
        return _dot_nt(q2, k_ref[rows_of(j), :])

    def update(s_scr, p_scr, j, diagonal):
        for r0 in range(0, 2 * tq, ATTN_STRIP):
            rows = slice(r0, r0 + ATTN_STRIP)
            s = s_scr[rows, :]
            if diagonal:
                qpos = r0 % tq + lax.broadcasted_iota(jnp.int32, s.shape, 0)
                kpos = lax.broadcasted_iota(jnp.int32, s.shape, 1)
                s = jnp.where(kpos <= qpos, s, -jnp.inf)
            chunks = [s[:, c:c + LANES] for c in range(0, tq, LANES)]
            m_prev = m_scr[rows, :]
            m_new = jnp.maximum(m_prev, jnp.max(_tree(jnp.maximum, chunks), axis=-1, keepdims=True))
            alpha = jnp.exp2(m_prev - m_new)
            probs = [jnp.exp2(ch - m_new) for ch in chunks]
            l_scr[rows, :] = alpha * l_scr[rows, :] + _tree(jnp.add, probs)
            acc_scr[rows, :] = alpha * acc_scr[rows, :]
            m_scr[rows, :] = m_new
            p_scr[rows, :] = jnp.concatenate([x.astype(BF16) for x in probs], axis=1)
        acc_scr[...] += _dot(p_scr[...], v_ref[rows_of(j), :])

    bufs = ((sa_scr, pa_scr), (sb_scr, pb_scr))

    def run(first, count, last_is_diagonal, prefetch_after):
        for i in range(count):
            if i + 1 < count or prefetch_after:
                bufs[(i + 1) % 2][0][...] = scores(first + i + 1)
            update(*bufs[i % 2], first + i, last_is_diagonal and i + 1 == count)

    sa_scr[...] = scores(0)

    def full_blocks(jj, carry):
        run(ATTN_UNROLL * jj, ATTN_UNROLL, False, True)
        return carry

    lax.fori_loop(0, qi // ATTN_UNROLL, full_blocks, 0)

    for rem in range(ATTN_UNROLL):
        @pl.when(qi % ATTN_UNROLL == rem)
        def _(rem=rem):
            run(qi - rem, rem + 1, True, False)

    o_ref[...] = _finish_softmax(lam_ref, lam_init, l_scr, acc_scr)


def _prompt_attention(lam_p, q, k, v, lam_init, tq):
    nb, s, _ = q.shape
    return pl.pallas_call(
        functools.partial(_prompt_attn_kernel, tq=tq, lam_init=lam_init),
        grid=(nb, N_HEADS, s // tq),
        in_specs=[_const_spec(lam_p.shape),
                  pl.BlockSpec((None, tq, V_HEAD_DIM), lambda b, h, i: (b, i, h)),
                  pl.BlockSpec((None, s, V_HEAD_DIM), lambda b, h, i: (b, 0, h)),
                  pl.BlockSpec((None, s, V_HEAD_DIM), lambda b, h, i: (b, 0, h))],
        out_specs=pl.BlockSpec((None, tq, V_HEAD_DIM), lambda b, h, i: (b, i, h)),
        out_shape=jax.ShapeDtypeStruct((nb, s, W_ATT), F32),
        scratch_shapes=[pltpu.VMEM((2 * tq, tq), F32), pltpu.VMEM((2 * tq, tq), F32),
                        pltpu.VMEM((2 * tq, tq), BF16), pltpu.VMEM((2 * tq, tq), BF16),
                        pltpu.VMEM((2 * tq, LANES), F32), pltpu.VMEM((2 * tq, LANES), F32),
                        pltpu.VMEM((2 * tq, V_HEAD_DIM), F32)],
        compiler_params=_params("parallel", "parallel", "arbitrary"),
        name="prompt_attention",
    )(lam_p, q, k, v)


def _sample_attn_kernel(pt_ref, lam_ref, bias_ref, bias_new_ref, q_ref, kn_ref, vn_ref, *refs,
                        n_steps, n_group, lam_init):
    kc_refs, vc_refs = refs[:n_group], refs[n_group:2 * n_group]
    o_ref, m_scr, l_scr, acc_scr = refs[2 * n_group:]
    j = pl.program_id(1)
    q2 = _split_branches(q_ref[...])

    @pl.when(j == 0)
    def _():
        _init_softmax(m_scr, l_scr, acc_scr)

    def flat(ref):
        page, heads, d = ref.shape
        return ref[...].reshape(page * heads, d).astype(BF16)

    bias = bias_ref[...]
    scores = [_dot_nt(q2, flat(kc)) + bias for kc in kc_refs]
    _online_softmax_step(scores, [flat(vc) for vc in vc_refs], m_scr, l_scr, acc_scr)

    @pl.when(j == n_steps - 1)
    def _():
        s_new = _dot_nt(q2, kn_ref[...].astype(BF16)) + bias_new_ref[...]
        _online_softmax_step([s_new], [vn_ref[...].astype(BF16)], m_scr, l_scr, acc_scr)
        o_ref[...] = _finish_softmax(lam_ref, lam_init, l_scr, acc_scr)


def _head_bias(n_rows, n_cols, t_new, n_valid):
    r = jnp.arange(n_rows)[:, None]
    c = jnp.arange(n_cols)[None, :]
    ok = (r % N_HEADS == c % N_HEADS) & (c < n_valid)
    if t_new is not None:
        ok &= c // N_HEADS <= (r // N_HEADS) % t_new
    return jnp.where(ok, 0.0, -jnp.inf).astype(F32)


def _sample_attention(page_table, lam_p, q, k_new, v_new, cache_k, cache_v, layer, lam_init):
    nb, n_pages = page_table.shape
    page = cache_k.shape[2]
    n_new = q.shape[1]
    t_new = n_new // N_HEADS
    n_group = PAGES_PER_STEP
    n_steps = n_pages // n_group
    rows = 2 * n_new
    pad = ((0, 0), (0, LANES - n_new), (0, 0))
    k_new, v_new = jnp.pad(k_new, pad), jnp.pad(v_new, pad)
    bias = _head_bias(rows, page * N_HEADS, None, page * N_HEADS)
    bias_new = _head_bias(rows, LANES, t_new, n_new)
    per_b = lambda b, j, pt: (b, 0, 0)
    const = lambda b, j, pt: (0, 0)

    def page_spec(g):
        return pl.BlockSpec((None, None, page, N_HEADS, V_HEAD_DIM),
                            lambda b, j, pt: (layer, pt[b, j * n_group + g], 0, 0, 0))

    new_spec = pl.BlockSpec((None, LANES, V_HEAD_DIM), per_b)
    pages = [page_spec(g) for g in range(n_group)]
    return pl.pallas_call(
        functools.partial(_sample_attn_kernel, n_steps=n_steps, n_group=n_group, lam_init=lam_init),
        grid_spec=pltpu.PrefetchScalarGridSpec(
            num_scalar_prefetch=1,
            grid=(nb, n_steps),
            in_specs=[pl.BlockSpec(lam_p.shape, const), pl.BlockSpec(bias.shape, const),
                      pl.BlockSpec(bias_new.shape, const),
                      pl.BlockSpec((None, n_new, V_HEAD_DIM), per_b), new_spec, new_spec] + pages + pages,
            out_specs=pl.BlockSpec((None, n_new, V_HEAD_DIM), per_b),
            scratch_shapes=[pltpu.VMEM((rows, LANES), F32), pltpu.VMEM((rows, LANES), F32),
                            pltpu.VMEM((rows, V_HEAD_DIM), F32)]),
        out_shape=jax.ShapeDtypeStruct((nb, n_new, V_HEAD_DIM), F32),
        compiler_params=_params("parallel", "arbitrary"),
        name="sample_attention",
    )(page_table, lam_p, bias, bias_new, q, k_new, v_new, *([cache_k] * n_group), *([cache_v] * n_group))


def _out_kernel(o_ref, ag_ref, sg_ref, mcp_ref, x_ref, subg_ref, watt_ref, wout_ref, fg_ref, y_ref,
                *, lam_init, final):
    heads = []
    for hd in range(N_HEADS):
        oh = o_ref[:, hd * V_HEAD_DIM:(hd + 1) * V_HEAD_DIM]
        heads.append(_rms_rows(oh, subg_ref[...]) * (1.0 - lam_init))
    o = jnp.concatenate(heads, axis=-1) * ag_ref[...].astype(F32)
    br_att = _dot(o.astype(BF16), watt_ref[...])
    merged = sg_ref[...].astype(F32) * br_att + mcp_ref[...]
    y = x_ref[...] + _dot(merged.astype(BF16), wout_ref[...])
    y_ref[...] = _rms_rows(y, fg_ref[...]) if final else y


def _out_proj(o, ag, sg, mcp, x, subln_g, w_att_up, w_out, final_g, layer, lam_init, final, tm):
    n = x.shape[0]
    row = lambda i: (i, 0)
    blk = pl.BlockSpec((tm, D_MODEL), row)
    return pl.pallas_call(
        functools.partial(_out_kernel, lam_init=lam_init, final=final),
        grid=(n // tm,),
        in_specs=[blk] * 5 + [_layer_spec(subln_g, layer), _layer_spec(w_att_up, layer),
                              _layer_spec(w_out, layer), _const_spec(final_g.shape)],
        out_specs=blk,
        out_shape=jax.ShapeDtypeStruct((n, D_MODEL), F32),
        compiler_params=_params("parallel"),
        name="out_proj",
    )(o, ag, sg, mcp, x, subln_g, w_att_up, w_out, final_g)


def _rope_tables(pos):
    half = HEAD_DIM // 2
    inv = ROPE_THETA ** (-jnp.arange(half, dtype=F32) / half)
    ang = pos.astype(F32)[:, None] * inv[None, :]
    cos, sin = jnp.cos(ang), jnp.sin(ang)
    reps = LANES // HEAD_DIM
    return (jnp.tile(jnp.concatenate([cos, cos], axis=-1), (1, reps)),
            jnp.tile(jnp.concatenate([-sin, sin], axis=-1), (1, reps)))


def kernel(x_prompt, x_sample, cache_k, cache_v, state_conv, state_pool, page_table, norm_g, w_in, conv_w, conv_b,
           conv_ln_g, conv_ln_b, conv_pw, lambda_q1, lambda_k1, lambda_q2, lambda_k2, subln_g, pool_w, pool_b,
           pool_scale, w_conv_up, w_att_up, w_pool_up, w_out, final_g):
    bp, s, _ = x_prompt.shape
    bd, t, _ = x_sample.shape
    depth = w_in.shape[0]
    past = page_table.shape[1] * cache_k.shape[2]
    cos_p, sin_p = _rope_tables(jnp.arange(s, dtype=jnp.int32))
    cos_s, sin_s = _rope_tables(jnp.tile(past + jnp.arange(t, dtype=jnp.int32), bd))
    conv0 = jnp.zeros((bp, CONV_BUF, W_CONV), F32)
    pool0 = jnp.zeros((bp, POOL_BUF, W_POOL), F32)
    fg = final_g.reshape(1, D_MODEL)

    rows = lambda a: a.reshape(depth, 1, -1)
    ng, sub_g = rows(norm_g), rows(subln_g)
    w_in_b, watt, wo = w_in.astype(BF16), w_att_up.astype(BF16), w_out.astype(BF16)
    p = dict(conv_w=conv_w, conv_b=rows(conv_b), conv_ln_g=rows(conv_ln_g), conv_ln_b=rows(conv_ln_b),
             conv_pw=conv_pw.astype(BF16), w_conv_up=w_conv_up.astype(BF16), pool_w=pool_w.astype(BF16),
             pool_b=rows(pool_b), pool_scale=rows(pool_scale), w_pool_up=w_pool_up.astype(BF16))

    xp, xs = x_prompt, x_sample
    k_all = v_all = None
    outs = [[] for _ in range(6)]
    for l in range(depth):
        lam_init = 0.8 - 0.6 * math.exp(-0.3 * l)
        final = l == depth - 1
        lam_p = jnp.stack([lambda_q1[l], lambda_k1[l], lambda_q2[l], lambda_k2[l]])

        q, k_all, kb, v_all, vb, ag, sg = _qkv_proj(xp, ng, w_in_b, cos_p, sin_p, l, tm=512,
                                                    kv_stack=(k_all, v_all))
        mcp, c_n, pl_n = _conv_pool(xp, ng, w_in_b, conv0, pool0, p, l, tm=512, pos_base=0)
        o = _prompt_attention(lam_p, q, kb, vb, lam_init, tq=512)
        flat = lambda a: a.reshape(bp * s, D_MODEL)
        xp = _out_proj(flat(o), flat(ag), flat(sg), flat(mcp), flat(xp), sub_g, watt, wo, fg,
                       l, lam_init, final, tm=512).reshape(bp, s, D_MODEL)
        outs[0].append(c_n)
        outs[1].append(pl_n)

        xs_rows = xs.reshape(1, bd * t, D_MODEL)
        q, k, kb, v, vb, ag, sg = _qkv_proj(xs_rows, ng, w_in_b, cos_s, sin_s, l, tm=bd * t)
        mcp, c_n, pl_n = _conv_pool(xs, ng, w_in_b, state_conv[l], state_pool[l], p, l, tm=t, pos_base=past)
        per_seq = lambda a: a.reshape(bd, t * N_HEADS, V_HEAD_DIM)
        o = _sample_attention(page_table, lam_p, per_seq(q), per_seq(k), per_seq(v), cache_k, cache_v, l, lam_init)
        flat = lambda a: a.reshape(bd * t, D_MODEL)
        xs = _out_proj(flat(o), flat(ag), flat(sg), flat(mcp), flat(xs), sub_g, watt, wo, fg,
                       l, lam_init, final, tm=bd * t).reshape(bd, t, D_MODEL)
        outs[2].append(k.reshape(bd, t, N_HEADS, V_HEAD_DIM))
        outs[3].append(v.reshape(bd, t, N_HEADS, V_HEAD_DIM))
        outs[4].append(c_n)
        outs[5].append(pl_n)

    st = [jnp.stack(o) for o in outs]
    return (xp, xs, k_all, v_all, st[0], st[1], st[2], st[3], st[4], st[5])
```
